```python
import math
import jax, jax.numpy as jnp
from jax import lax
import numpy as np

D_MODEL = 2048
BATCH = 8
SEQ = 4096
DEPTH = 4

N_META = 16
MIX_WIDTH = D_MODEL
SB_HEADS = 8
SB_W = MIX_WIDTH // 2
SB_HEAD_DIM = SB_W // SB_HEADS
SB_BLOCK = 128
GLA_HEADS = 4
GLA_VW = MIX_WIDTH - SB_W
GLA_DV = GLA_VW // GLA_HEADS
GLA_KW = GLA_VW // 2
GLA_DK = GLA_KW // GLA_HEADS
GLA_GATE_RANK = 16
GLA_GATE_TAU = 16.0
GLA_CHUNK = 64
N_EXPERTS = 16
N_GROUPS = 4
EXPERTS_PER_GROUP = N_EXPERTS // N_GROUPS
TOP_K = 2
D_FF = D_MODEL // 2
MOE_BLOCK = 256
LN_EPS = 1e-5
RMS_EPS = 1e-6
DN_ALPHA = (2 * DEPTH) ** 0.25
DN_BETA = (8 * DEPTH) ** -0.25
IN_SPLITS = (SB_W, SB_W, SB_W, GLA_KW, GLA_KW, GLA_VW, GLA_GATE_RANK, GLA_VW)
IN_WIDTH = sum(IN_SPLITS)

kernel_name = "hybrid_sb_gla_groupmoe_deepnorm"


def layer_norm(x, g, b):
    xf = x.astype(jnp.float32)
    mu = jnp.mean(xf, -1, keepdims=True)
    var = jnp.mean(jnp.square(xf - mu), -1, keepdims=True)
    return ((xf - mu) * lax.rsqrt(var + LN_EPS) * g + b).astype(x.dtype)


def head_rms_norm(o, g):
    b_, h, l, d = o.shape
    of = o.astype(jnp.float32)
    of = of * lax.rsqrt(jnp.mean(of * of, -1, keepdims=True) + RMS_EPS)
    of = of * g.reshape(h, 1, d)
    return of.transpose(0, 2, 1, 3).reshape(b_, l, h * d).astype(o.dtype)


def split_heads(t, n_heads):
    b_, l, w = t.shape
    return t.reshape(b_, l, n_heads, w // n_heads).transpose(0, 2, 1, 3)


def stick_breaking_attention(q, k, v):
    b_, h, l, d = q.shape
    pad = (-l) % SB_BLOCK
    padw = ((0, 0), (0, 0), (pad, 0), (0, 0))
    q, k, v = jnp.pad(q, padw), jnp.pad(k, padw), jnp.pad(v, padw)
    n_pos = l + pad
    scale = d ** -0.5
    outs = []
    for i in range(n_pos // SB_BLOCK):
        t0 = i * SB_BLOCK
        kn = t0 + SB_BLOCK
        z = jnp.einsum('bhtd,bhsd->bhts', q[:, :, t0:kn], k[:, :, :kn],
                       preferred_element_type=jnp.float32) * scale
        t_pos = t0 + jnp.arange(SB_BLOCK)[:, None]
        s_pos = jnp.arange(kn)[None, :]
        valid = (s_pos < t_pos) & (s_pos >= pad)
        u = jnp.where(valid, jax.nn.softplus(z), 0.0)
        between = lax.cumsum(u, axis=3, reverse=True) - u
        w = jnp.where(valid, jnp.exp(jax.nn.log_sigmoid(z) - between), 0.0)
        outs.append(jnp.einsum('bhts,bhsd->bhtd', w.astype(v.dtype), v[:, :, :kn]))
    return jnp.concatenate(outs, axis=2)[:, :, pad:]


def gla_chunked(q, k, v, log_a):
    b_, h, l, dk = q.shape
    dv = v.shape[-1]
    out_dtype = v.dtype
    pad = (-l) % GLA_CHUNK
    padw = ((0, 0), (0, 0), (pad, 0), (0, 0))
    q, k, v, log_a = (jnp.pad(a, padw) for a in (q, k, v, log_a))
    c = GLA_CHUNK
    n = (l + pad) // c
    f32 = jnp.float32
    q = q.reshape(b_, h, n, c, dk).astype(f32) * (dk ** -0.5)
    k = k.reshape(b_, h, n, c, dk).astype(f32)
    v = v.reshape(b_, h, n, c, dv).astype(f32)
    g = log_a.reshape(b_, h, n, c, dk).astype(f32)
    bcum = jnp.cumsum(g, axis=3)
    b_last = bcum[:, :, :, -1:]
    q_dec = q * jnp.exp(bcum)
    k_inv = k * jnp.exp(-bcum)
    causal = jnp.tril(jnp.ones((c, c), bool))
    scores = jnp.where(causal, jnp.einsum('bhnck,bhnsk->bhncs', q_dec, k_inv), 0.0)
    o_intra = jnp.einsum('bhncs,bhnsv->bhncv', scores, v)
    incr = jnp.einsum('bhnck,bhncv->bhnkv', k * jnp.exp(b_last - bcum), v)
    decay = jnp.exp(b_last[:, :, :, 0])

    def step(state, inp):
        dec, inc = inp
        return dec[..., None] * state + inc, state

    s0 = jnp.zeros((b_, h, dk, dv), f32)
    _, s_prev = lax.scan(step, s0, (jnp.moveaxis(decay, 2, 0), jnp.moveaxis(incr, 2, 0)))
    s_prev = jnp.moveaxis(s_prev, 0, 2)
    o_inter = jnp.einsum('bhnck,bhnkv->bhncv', q_dec, s_prev)
    o = (o_intra + o_inter).reshape(b_, h, n * c, dv)[:, :, pad:]
    return o.astype(out_dtype)


def group_route(x_flat, w_router, b_router):
    t = x_flat.shape[0]
    logits = (x_flat @ w_router).astype(jnp.float32) + b_router.astype(jnp.float32)
    aff = jax.nn.softmax(logits, axis=-1)
    grouped = aff.reshape(t, N_GROUPS, EXPERTS_PER_GROUP)
    group_score = jnp.sum(lax.top_k(grouped, TOP_K)[0], axis=-1)
    g_sel = jnp.argmax(group_score, axis=-1)
    in_group = (jnp.arange(N_EXPERTS) // EXPERTS_PER_GROUP)[None, :] == g_sel[:, None]
    top_w, top_e = lax.top_k(jnp.where(in_group, aff, -1.0), TOP_K)
    top_w = top_w / jnp.sum(top_w, axis=-1, keepdims=True)
    return top_w, top_e


def moe_ffn(x_flat, top_w, top_e, w_gate, w_up, w_down):
    t, d = x_flat.shape
    n_assign = t * TOP_K
    e_flat = top_e.reshape(n_assign)
    tok_flat = jnp.repeat(jnp.arange(t, dtype=jnp.int32), TOP_K)
    w_flat = top_w.reshape(n_assign)
    order = jnp.argsort(e_flat)
    e_s, tok_s, w_s = e_flat[order], tok_flat[order], w_flat[order]
    counts = jax.ops.segment_sum(jnp.ones((n_assign,), jnp.int32), e_flat, N_EXPERTS)
    padded = (counts + MOE_BLOCK - 1) // MOE_BLOCK * MOE_BLOCK
    start = jnp.cumsum(counts) - counts
    pend = jnp.cumsum(padded)
    pstart = pend - padded
    dest = pstart[e_s] + jnp.arange(n_assign, dtype=jnp.int32) - start[e_s]
    n_blocks = -(-n_assign // MOE_BLOCK) + N_EXPERTS
    n_rows = n_blocks * MOE_BLOCK
    row_tok = jnp.full((n_rows,), t, jnp.int32).at[dest].set(tok_s)
    row_w = jnp.zeros((n_rows,), w_s.dtype).at[dest].set(w_s)
    block_e = jnp.minimum(
        jnp.searchsorted(pend, jnp.arange(n_blocks, dtype=jnp.int32) * MOE_BLOCK, side='right'),
        N_EXPERTS - 1)
    x_pad = jnp.concatenate([x_flat, jnp.zeros((1, d), x_flat.dtype)], axis=0)
    xb = x_pad[row_tok].reshape(n_blocks, MOE_BLOCK, d)

    def expert_block(args):
        xblk, e = args
        hid = jax.nn.silu(xblk @ w_gate[e]) * (xblk @ w_up[e])
        return hid @ w_down[e]

    yb = lax.map(expert_block, (xb, block_e)).reshape(n_rows, d)
    y = yb * row_w[:, None].astype(yb.dtype)
    return jnp.zeros((t + 1, d), y.dtype).at[row_tok].add(y)[:t]


def hybrid_layer(x, w_in, w_gate_lr, b_gate_lr, sb_out_norm, gla_out_norm, w_out,
                 ln1_g, ln1_b, w_router, b_router, w_ff_gate, w_ff_up, w_ff_down, ln2_g, ln2_b):
    b_, l, d = x.shape
    proj = x @ w_in
    splits = np.cumsum(IN_SPLITS)[:-1].tolist()
    sq, sk, sv, gq, gk, gv, g_lr, g_out = jnp.split(proj, splits, axis=-1)
    o_sb = stick_breaking_attention(split_heads(sq, SB_HEADS), split_heads(sk, SB_HEADS),
                                    split_heads(sv, SB_HEADS))
    log_a = jax.nn.log_sigmoid((g_lr @ w_gate_lr + b_gate_lr).astype(jnp.float32)) / GLA_GATE_TAU
    o_gla = gla_chunked(split_heads(gq, GLA_HEADS), split_heads(gk, GLA_HEADS),
                        split_heads(gv, GLA_HEADS), split_heads(log_a, GLA_HEADS))
    y_sb = head_rms_norm(o_sb, sb_out_norm)
    y_gla = head_rms_norm(o_gla, gla_out_norm) * jax.nn.silu(g_out)
    mix = jnp.concatenate([y_sb, y_gla], axis=-1) @ w_out
    x = layer_norm(DN_ALPHA * x + mix, ln1_g, ln1_b)
    x_flat = x.reshape(b_ * l, d)
    top_w, top_e = group_route(x_flat, w_router, b_router)
    ffn = moe_ffn(x_flat, top_w, top_e, w_ff_gate, w_ff_up, w_ff_down).reshape(b_, l, d)
    return layer_norm(DN_ALPHA * x + ffn, ln2_g, ln2_b)


def setup_inputs(seed: int = 0) -> dict:
    key = jax.random.key(seed)
    ks = jax.random.split(key, 18)
    f32 = jnp.float32

    def nrm(k, shape, s):
        return jax.random.normal(k, shape, f32) * s

    return {
        'x': nrm(ks[0], (BATCH, SEQ, D_MODEL), 1.0),
        'meta_tokens': nrm(ks[1], (N_META, D_MODEL), 1.0),
        'w_in': nrm(ks[2], (DEPTH, D_MODEL, IN_WIDTH), D_MODEL ** -0.5),
        'w_gate_lr': nrm(ks[3], (DEPTH, GLA_GATE_RANK, GLA_KW), GLA_GATE_RANK ** -0.5),
        'b_gate_lr': nrm(ks[4], (DEPTH, GLA_KW), 0.1),
        'sb_out_norm': 1.0 + nrm(ks[5], (DEPTH, SB_W), 0.02),
        'gla_out_norm': 1.0 + nrm(ks[6], (DEPTH, GLA_VW), 0.02),
        'w_out': nrm(ks[7], (DEPTH, MIX_WIDTH, D_MODEL), MIX_WIDTH ** -0.5 * DN_BETA),
        'ln1_g': 1.0 + nrm(ks[8], (DEPTH, D_MODEL), 0.02),
        'ln1_b': nrm(ks[9], (DEPTH, D_MODEL), 0.02),
        'w_router': nrm(ks[10], (D_MODEL, N_EXPERTS), D_MODEL ** -0.5),
        'b_router': nrm(ks[11], (N_EXPERTS,), 0.01),
        'w_ff_gate': nrm(ks[12], (DEPTH, N_EXPERTS, D_MODEL, D_FF), D_MODEL ** -0.5),
        'w_ff_up': nrm(ks[13], (DEPTH, N_EXPERTS, D_MODEL, D_FF), D_MODEL ** -0.5),
        'w_ff_down': nrm(ks[14], (DEPTH, N_EXPERTS, D_FF, D_MODEL), D_FF ** -0.5 * DN_BETA),
        'ln2_g': 1.0 + nrm(ks[15], (DEPTH, D_MODEL), 0.02),
        'ln2_b': nrm(ks[16], (DEPTH, D_MODEL), 0.02),
    }


def reference(x, meta_tokens, w_in, w_gate_lr, b_gate_lr, sb_out_norm, gla_out_norm, w_out,
              ln1_g, ln1_b, w_router, b_router, w_ff_gate, w_ff_up, w_ff_down, ln2_g, ln2_b):
    b_ = x.shape[0]
    meta = jnp.broadcast_to(meta_tokens[None].astype(x.dtype), (b_, N_META, x.shape[-1]))
    h = jnp.concatenate([meta, x], axis=1)
    for layer in range(DEPTH):
        h = hybrid_layer(h, w_in[layer], w_gate_lr[layer], b_gate_lr[layer], sb_out_norm[layer],
                         gla_out_norm[layer], w_out[layer], ln1_g[layer], ln1_b[layer],
                         w_router, b_router, w_ff_gate[layer], w_ff_up[layer], w_ff_down[layer],
                         ln2_g[layer], ln2_b[layer])
    return h[:, N_META:]
```

```python
import functools
import math

import jax
import jax.numpy as jnp
from jax import lax
from jax.experimental import pallas as pl
from jax.experimental.pallas import tpu as pltpu

f32 = jnp.float32
bf16 = jnp.bfloat16
u32 = jnp.uint32
i32 = jnp.int32

N_META = 16
SB_HEADS = 8
SB_HEAD_DIM = 128
GLA_HEADS = 4
GLA_DK = 128
GLA_DV = 256
GLA_GATE_RANK = 16
GLA_GATE_TAU = 16.0
GLA_CHUNK = 64
N_EXPERTS = 16
N_GROUPS = 4
EXPERTS_PER_GROUP = 4
LN_EPS = 1e-5
RMS_EPS = 1e-6

LANES = 128
VMEM_LIMIT = 56 * 1024 * 1024

SEQ_ALIGN = 256
SB_TQ = 256
SB_TK = 256
MOE_BLK = 512
TOK_TILE = 256
RANK_CHUNK = 512
SB_DEAD_LOGW = 104.0


def _pick(n, cands):
    for c in cands:
        if n % c == 0:
            return c
    raise ValueError(f"no tile in {cands} divides {n}")


def _cparams(sem, vmem=VMEM_LIMIT):
    return pltpu.CompilerParams(dimension_semantics=sem, vmem_limit_bytes=vmem)


def _dot(a, b):
    return jnp.dot(a, b, preferred_element_type=f32)


def _dot_nt(a, b):
    return lax.dot_general(a, b, (((1,), (1,)), ((), ())), preferred_element_type=f32)


def _dot_tn(a, b):
    return lax.dot_general(a, b, (((0,), (0,)), ((), ())), preferred_element_type=f32)


def _pack_pair(lo, hi):
    lo_b = lax.bitcast_convert_type(lo.astype(bf16).astype(f32), u32)
    hi_b = lax.bitcast_convert_type(hi.astype(bf16).astype(f32), u32)
    return (hi_b & jnp.uint32(0xFFFF0000)) | (lo_b >> 16)


def _unpack_pair(w):
    lo = lax.bitcast_convert_type(w << 16, f32)
    hi = lax.bitcast_convert_type(w & jnp.uint32(0xFFFF0000), f32)
    return lo, hi


def _mm_kernel(x_ref, w_ref, o_ref):
    o_ref[...] = _dot(x_ref[...], w_ref[...]).astype(o_ref.dtype)


def _matmul(x, w, out_dtype):
    m, k = x.shape
    n = w.shape[1]
    tm = _pick(m, (1024, 512, 256, 128))
    tn = _pick(n, (512, 256, 128))
    return pl.pallas_call(
        _mm_kernel,
        grid=(m // tm, n // tn),
        in_specs=[pl.BlockSpec((tm, k), lambda i, j: (i, 0)),
                  pl.BlockSpec((k, tn), lambda i, j: (0, j))],
        out_specs=pl.BlockSpec((tm, tn), lambda i, j: (i, j)),
        out_shape=jax.ShapeDtypeStruct((m, n), out_dtype),
        compiler_params=_cparams(("parallel", "parallel")),
    )(x, w)


def _gate_kernel(x_ref, wlr_ref, wg_ref, b_ref, o_ref):
    g_lr = _dot(x_ref[...], wlr_ref[...]).astype(bf16)
    y = _dot(g_lr, wg_ref[...]) + b_ref[...]
    log_sig = jnp.minimum(y, 0.0) - jnp.log1p(jnp.exp(-jnp.abs(y)))
    o_ref[...] = log_sig * (1.0 / GLA_GATE_TAU)


def _gla_gate(x, w_lr, w_g, b_g):
    m, k = x.shape
    n = w_g.shape[1]
    tm = _pick(m, (512, 256, 128))
    return pl.pallas_call(
        _gate_kernel,
        grid=(m // tm,),
        in_specs=[pl.BlockSpec((tm, k), lambda i: (i, 0)),
                  pl.BlockSpec((k, LANES), lambda i: (0, 0)),
                  pl.BlockSpec((LANES, n), lambda i: (0, 0)),
                  pl.BlockSpec((1, n), lambda i: (0, 0))],
        out_specs=pl.BlockSpec((tm, n), lambda i: (i, 0)),
        out_shape=jax.ShapeDtypeStruct((m, n), f32),
        compiler_params=_cparams(("parallel",)),
    )(x, w_lr, w_g, b_g)


def _sb_kernel(q_ref, k_ref, v_ref, g_ref, o_ref, *, tq, tk, pad):
    i = pl.program_id(2)
    q = q_ref[0]
    row0 = i * tq
    t_pos = row0 + lax.broadcasted_iota(i32, (tq, tk), 0)
    col = lax.broadcasted_iota(i32, (tq, tk), 1)
    later = (lax.broadcasted_iota(i32, (tk, tk), 0) > lax.broadcasted_iota(i32, (tk, tk), 1)).astype(bf16)
    scale = SB_HEAD_DIM ** -0.5

    def cond(c):
        jb, _, _, cmin = c
        return jnp.logical_and(jb >= 0, cmin < SB_DEAD_LOGW)

    def body(c):
        jb, carry, acc, _ = c
        ks = pl.multiple_of(jb * tk, tk)
        kb = k_ref[0, pl.ds(ks, tk), :]
        vb = v_ref[0, pl.ds(ks, tk), :]
        z = _dot_nt(q, kb) * scale
        s_pos = ks + col
        valid = jnp.logical_and(s_pos < t_pos, s_pos >= pad)
        sp = jnp.maximum(z, 0.0) + jnp.log1p(jnp.exp(-jnp.abs(z)))
        u = jnp.where(valid, sp, 0.0)
        between = _dot(u.astype(bf16), later) + carry
        w = jnp.where(valid, jnp.exp(z - sp - between), 0.0)
        acc = acc + _dot(w.astype(bf16), vb)
        carry = carry + jnp.sum(u, axis=1, keepdims=True)
        return jb - 1, carry, acc, jnp.min(carry)

    n_blocks = (row0 + tq) // tk
    init = (n_blocks - 1, jnp.zeros((tq, 1), f32), jnp.zeros((tq, SB_HEAD_DIM), f32), jnp.float32(0.0))
    _, _, acc, _ = lax.while_loop(cond, body, init)
    ms = jnp.mean(acc * acc, axis=1, keepdims=True)
    o_ref[0] = (acc * lax.rsqrt(ms + RMS_EPS) * g_ref[...]).astype(o_ref.dtype)


def _sb_attention(proj, norm_g, pad):
    b, lp, _ = proj.shape
    tq, tk = SB_TQ, SB_TK
    kern = functools.partial(_sb_kernel, tq=tq, tk=tk, pad=pad)
    return pl.pallas_call(
        kern,
        grid=(b, SB_HEADS, lp // tq),
        in_specs=[pl.BlockSpec((1, tq, SB_HEAD_DIM), lambda bi, h, i: (bi, i, h)),
                  pl.BlockSpec((1, lp, SB_HEAD_DIM), lambda bi, h, i: (bi, 0, SB_HEADS + h)),
                  pl.BlockSpec((1, lp, SB_HEAD_DIM), lambda bi, h, i: (bi, 0, 2 * SB_HEADS + h)),
                  pl.BlockSpec((1, SB_HEAD_DIM), lambda bi, h, i: (0, h))],
        out_specs=pl.BlockSpec((1, tq, SB_HEAD_DIM), lambda bi, h, i: (bi, i, h)),
        out_shape=jax.ShapeDtypeStruct((b, lp, SB_HEADS * SB_HEAD_DIM), bf16),
        compiler_params=_cparams(("parallel", "parallel", "parallel")),
    )(proj, proj, proj, norm_g)


def _gla_kernel(q_ref, k_ref, v_ref, la_ref, go_ref, gn_ref, o_ref, st_ref, *, pad, n_chunks):
    c = GLA_CHUNK
    st_ref[...] = jnp.zeros_like(st_ref)
    causal = lax.broadcasted_iota(i32, (c, c), 1) <= lax.broadcasted_iota(i32, (c, c), 0)
    causal_b = causal.astype(bf16)
    row_iota = lax.broadcasted_iota(i32, (c, 1), 0)
    scale = GLA_DK ** -0.5

    def body(n, carry):
        r0 = pl.multiple_of(n * c, c)
        real = (r0 + row_iota) >= pad
        g = jnp.where(real, la_ref[0, pl.ds(r0, c), :], 0.0)
        g_hi = g.astype(bf16)
        g_lo = (g - g_hi.astype(f32)).astype(bf16)
        bcum = _dot(causal_b, g_hi) + _dot(causal_b, g_lo)
        b_last = bcum[c - 1:c, :]
        qf = q_ref[0, pl.ds(r0, c), :].astype(f32) * scale
        kf = jnp.where(real, k_ref[0, pl.ds(r0, c), :].astype(f32), 0.0)
        vb = v_ref[0, pl.ds(r0, c), :]
        q_dec = (qf * jnp.exp(bcum)).astype(bf16)
        k_inv = (kf * jnp.exp(-bcum)).astype(bf16)
        k_dec = (kf * jnp.exp(b_last - bcum)).astype(bf16)
        scores = jnp.where(causal, _dot_nt(q_dec, k_inv), 0.0)
        st = st_ref[...]
        o = _dot(scores.astype(bf16), vb) + _dot_nt(q_dec, st.astype(bf16))
        st_ref[...] = st * jnp.exp(b_last) + _dot_tn(vb, k_dec)
        ms = jnp.mean(o * o, axis=1, keepdims=True)
        gate = go_ref[0, pl.ds(r0, c), :].astype(f32)
        y = o * lax.rsqrt(ms + RMS_EPS) * gn_ref[...] * (gate * jax.nn.sigmoid(gate))
        o_ref[0, pl.ds(r0, c), :] = y.astype(o_ref.dtype)
        return carry

    lax.fori_loop(0, n_chunks, body, 0)


def _gla(proj, log_a, norm_g, pad):
    b, lp, _ = proj.shape
    kern = functools.partial(_gla_kernel, pad=pad, n_chunks=lp // GLA_CHUNK)
    return pl.pallas_call(
        kern,
        grid=(b, GLA_HEADS),
        in_specs=[pl.BlockSpec((1, lp, GLA_DK), lambda bi, h: (bi, 0, 24 + h)),
                  pl.BlockSpec((1, lp, GLA_DK), lambda bi, h: (bi, 0, 28 + h)),
                  pl.BlockSpec((1, lp, GLA_DV), lambda bi, h: (bi, 0, 16 + h)),
                  pl.BlockSpec((1, lp, GLA_DK), lambda bi, h: (bi, 0, h)),
                  pl.BlockSpec((1, lp, GLA_DV), lambda bi, h: (bi, 0, 20 + h)),
                  pl.BlockSpec((1, GLA_DV), lambda bi, h: (0, h))],
        out_specs=pl.BlockSpec((1, lp, GLA_DV), lambda bi, h: (bi, 0, h)),
        out_shape=jax.ShapeDtypeStruct((b, lp, GLA_HEADS * GLA_DV), bf16),
        scratch_shapes=[pltpu.VMEM((GLA_DV, GLA_DK), f32)],
        compiler_params=_cparams(("parallel", "parallel")),
    )(proj, proj, proj, log_a, proj, norm_g)


def _layer_norm_rows(r, g, b):
    mu = jnp.mean(r, axis=-1, keepdims=True)
    d = r - mu
    var = jnp.mean(d * d, axis=-1, keepdims=True)
    return d * lax.rsqrt(var + LN_EPS) * g + b


def _route(logits):
    m = jnp.max(logits, axis=0, keepdims=True)
    e = jnp.exp(logits - m)
    aff = e / jnp.sum(e, axis=0, keepdims=True)
    rows = [aff[j:j + 1, :] for j in range(N_EXPERTS)]
    scores = []
    for g in range(N_GROUPS):
        a, b, c, d = rows[4 * g:4 * g + 4]
        hi1, lo1 = jnp.maximum(a, b), jnp.minimum(a, b)
        hi2, lo2 = jnp.maximum(c, d), jnp.minimum(c, d)
        scores.append(jnp.maximum(hi1, hi2) + jnp.maximum(jnp.minimum(hi1, hi2), jnp.maximum(lo1, lo2)))
    best = scores[0]
    g_sel = jnp.zeros(best.shape, i32)
    for g in range(1, N_GROUPS):
        better = scores[g] > best
        g_sel = jnp.where(better, g, g_sel)
        best = jnp.where(better, scores[g], best)
    sel = list(rows[0:4])
    for g in range(1, N_GROUPS):
        here = g_sel == g
        sel = [jnp.where(here, rows[4 * g + j], sel[j]) for j in range(4)]
    v1, i1 = sel[0], jnp.zeros(best.shape, i32)
    for j in range(1, 4):
        gt = sel[j] > v1
        i1 = jnp.where(gt, j, i1)
        v1 = jnp.where(gt, sel[j], v1)
    v2, i2 = jnp.full(best.shape, -1.0, f32), jnp.zeros(best.shape, i32)
    for j in range(4):
        cand = jnp.where(i1 == j, -1.0, sel[j])
        gt = cand > v2
        i2 = jnp.where(gt, j, i2)
        v2 = jnp.where(gt, cand, v2)
    s = v1 + v2
    top_e = jnp.concatenate([4 * g_sel + i1, 4 * g_sel + i2], axis=0)
    top_w = jnp.concatenate([v1 / s, v2 / s], axis=0)
    return top_e, top_w


def _oproj_kernel(ysb_ref, ygla_ref, w1_ref, w2_ref, h_ref, g_ref, b_ref, wr_ref, br_ref,
                  x1_ref, x1p_ref, te_ref, tw_ref, *, alpha):
    mix = _dot(ysb_ref[...], w1_ref[...]) + _dot(ygla_ref[...], w2_ref[...])
    x1 = _layer_norm_rows(alpha * h_ref[...] + mix, g_ref[...], b_ref[...])
    x1_ref[...] = x1
    half = x1.shape[1] // 2
    x1p_ref[...] = _pack_pair(x1[:, :half], x1[:, half:])
    xh = x1.astype(bf16)
    xl = (x1 - xh.astype(f32)).astype(bf16)
    wr = wr_ref[...]
    l_h = _dot_nt(wr, xh)
    l_l = _dot_nt(wr[:N_EXPERTS], xl)
    logits = l_h[:N_EXPERTS] + l_h[N_EXPERTS:] + l_l + br_ref[...]
    top_e, top_w = _route(logits)
    te_ref[...] = top_e
    tw_ref[...] = top_w


def _oproj_ln_route(y_sb, y_gla, w1, w2, h, ln_g, ln_b, wr, br, alpha):
    m, d = h.shape
    kh = y_sb.shape[1]
    tm = _pick(m, (512, 256, 128))
    row = lambda i: (i, 0)
    fix = lambda i: (0, 0)
    col = lambda i: (0, i)
    return pl.pallas_call(
        functools.partial(_oproj_kernel, alpha=alpha),
        grid=(m // tm,),
        in_specs=[pl.BlockSpec((tm, kh), row), pl.BlockSpec((tm, kh), row),
                  pl.BlockSpec((kh, d), fix), pl.BlockSpec((kh, d), fix),
                  pl.BlockSpec((tm, d), row),
                  pl.BlockSpec((1, d), fix), pl.BlockSpec((1, d), fix),
                  pl.BlockSpec((2 * N_EXPERTS, d), fix), pl.BlockSpec((N_EXPERTS, 1), fix)],
        out_specs=[pl.BlockSpec((tm, d), row), pl.BlockSpec((tm, d // 2), row),
                   pl.BlockSpec((2, tm), col), pl.BlockSpec((2, tm), col)],
        out_shape=[jax.ShapeDtypeStruct((m, d), f32), jax.ShapeDtypeStruct((m, d // 2), u32),
                   jax.ShapeDtypeStruct((2, m), i32), jax.ShapeDtypeStruct((2, m), f32)],
        compiler_params=_cparams(("parallel",)),
    )(y_sb, y_gla, w1, w2, h, ln_g, ln_b, wr, br)


def _rank_kernel(e_ref, rank_ref, cnt_ref, carry_ref, *, chunk):
    @pl.when(pl.program_id(0) == 0)
    def _():
        carry_ref[...] = jnp.zeros_like(carry_ref)

    e = e_ref[...]
    onehot = lax.broadcasted_iota(i32, (N_EXPERTS, chunk), 0) == e
    before = (lax.broadcasted_iota(i32, (chunk, chunk), 0)
              < lax.broadcasted_iota(i32, (chunk, chunk), 1)).astype(bf16)
    seen = _dot(onehot.astype(bf16), before) + carry_ref[...]
    rank_ref[...] = jnp.sum(jnp.where(onehot, seen, 0.0), axis=0, keepdims=True).astype(i32)
    total = carry_ref[...] + jnp.sum(onehot.astype(f32), axis=1, keepdims=True)
    carry_ref[...] = total
    cnt_ref[...] = total


def _expert_ranks(e_flat):
    n = e_flat.shape[1]
    chunk = _pick(n, (RANK_CHUNK, 256, 128))
    return pl.pallas_call(
        functools.partial(_rank_kernel, chunk=chunk),
        grid=(n // chunk,),
        in_specs=[pl.BlockSpec((1, chunk), lambda i: (0, i))],
        out_specs=[pl.BlockSpec((1, chunk), lambda i: (0, i)),
                   pl.BlockSpec((N_EXPERTS, 1), lambda i: (0, 0))],
        out_shape=[jax.ShapeDtypeStruct((1, n), i32), jax.ShapeDtypeStruct((N_EXPERTS, 1), f32)],
        scratch_shapes=[pltpu.VMEM((N_EXPERTS, 1), f32)],
        compiler_params=_cparams(("arbitrary",)),
    )(e_flat)


def _row_copy(src, dst, sem):
    return pltpu.make_async_copy(src, dst, sem)


def _dispatch_kernel(dest_hbm, x_ref, zeros_hbm, xs_hbm, dsm, sem_i, sem_d, *, tm):
    del zeros_hbm
    i = pl.program_id(0)
    idx_cp = _row_copy(dest_hbm.at[i], dsm, sem_i)
    idx_cp.start()
    idx_cp.wait()

    def body(r, carry):
        for k in range(2):
            d = dsm[0, k * tm + r]
            _row_copy(x_ref.at[pl.ds(r, 1)], xs_hbm.at[pl.ds(d, 1)], sem_d).start()
        return carry

    lax.fori_loop(0, tm, body, 0)
    for k in range(2):
        _row_copy(x_ref, xs_hbm.at[pl.ds(0, tm)], sem_d).wait()


def _dispatch(x_packed, dest_tiles, n_rows):
    m, w = x_packed.shape
    tm = dest_tiles.shape[2] // 2
    zeros = jnp.zeros((n_rows, w), u32)
    return pl.pallas_call(
        functools.partial(_dispatch_kernel, tm=tm),
        grid=(m // tm,),
        in_specs=[pl.BlockSpec(memory_space=pl.ANY),
                  pl.BlockSpec((tm, w), lambda i: (i, 0)),
                  pl.BlockSpec(memory_space=pl.ANY)],
        out_specs=pl.BlockSpec(memory_space=pl.ANY),
        out_shape=jax.ShapeDtypeStruct((n_rows, w), u32),
        scratch_shapes=[pltpu.SMEM((1, 2 * tm), i32), pltpu.SemaphoreType.DMA(()), pltpu.SemaphoreType.DMA(())],
        input_output_aliases={2: 0},
        compiler_params=_cparams(("arbitrary",)),
    )(dest_tiles, x_packed, zeros)


def _ffn_kernel(be_ref, na_ref, xs_ref, wg_ref, wu_ref, wd_ref, y_ref):
    del be_ref
    blk = pl.program_id(0)

    @pl.when(blk < na_ref[0])
    def _():
        lo, hi = _unpack_pair(xs_ref[...])
        lo, hi = lo.astype(bf16), hi.astype(bf16)
        half = lo.shape[1]
        gate = _dot(lo, wg_ref[0, :half, :]) + _dot(hi, wg_ref[0, half:, :])
        up = _dot(lo, wu_ref[0, :half, :]) + _dot(hi, wu_ref[0, half:, :])
        hid = (gate * jax.nn.sigmoid(gate) * up).astype(bf16)
        out = _dot(hid, wd_ref[0])
        y_ref[...] = _pack_pair(out[:, :half], out[:, half:])

    @pl.when(blk >= na_ref[0])
    def _():
        y_ref[...] = jnp.zeros_like(y_ref)


def _expert_ffn(xs, block_e, n_active, wg, wu, wd):
    n_rows, w = xs.shape
    d, dff = wg.shape[1], wg.shape[2]
    n_blocks = n_rows // MOE_BLK
    grid_spec = pltpu.PrefetchScalarGridSpec(
        num_scalar_prefetch=2,
        grid=(n_blocks,),
        in_specs=[pl.BlockSpec((MOE_BLK, w), lambda b, be, na: (b, 0)),
                  pl.BlockSpec((1, d, dff), lambda b, be, na: (be[b], 0, 0)),
                  pl.BlockSpec((1, d, dff), lambda b, be, na: (be[b], 0, 0)),
                  pl.BlockSpec((1, dff, d), lambda b, be, na: (be[b], 0, 0))],
        out_specs=pl.BlockSpec((MOE_BLK, w), lambda b, be, na: (b, 0)),
    )
    return pl.pallas_call(
        _ffn_kernel,
        grid_spec=grid_spec,
        out_shape=jax.ShapeDtypeStruct((n_rows, w), u32),
        compiler_params=_cparams(("arbitrary",)),
    )(block_e, n_active, xs, wg, wu, wd)


def _combine_kernel(dest_hbm, x1_ref, tw_ref, g_ref, b_ref, y_hbm, h32_ref, h16_ref,
                    dsm, ybuf, sem_i, sem_g, *, tm, alpha):
    i = pl.program_id(0)
    idx_cp = _row_copy(dest_hbm.at[i], dsm, sem_i)
    idx_cp.start()
    idx_cp.wait()

    def body(r, carry):
        for k in range(2):
            d = dsm[0, k * tm + r]
            _row_copy(y_hbm.at[pl.ds(d, 1)], ybuf.at[k, pl.ds(r, 1)], sem_g).start()
        return carry

    lax.fori_loop(0, tm, body, 0)
    for k in range(2):
        _row_copy(y_hbm.at[pl.ds(0, tm)], ybuf.at[k], sem_g).wait()

    tw = tw_ref[...]
    w0, w1 = tw[:, 0:1], tw[:, 1:2]
    lo0, hi0 = _unpack_pair(ybuf[0])
    lo1, hi1 = _unpack_pair(ybuf[1])
    half = lo0.shape[1]
    x1 = x1_ref[...]
    r_lo = alpha * x1[:, :half] + (w0 * lo0 + w1 * lo1)
    r_hi = alpha * x1[:, half:] + (w0 * hi0 + w1 * hi1)
    inv_d = 1.0 / (2 * half)
    mu = (jnp.sum(r_lo, axis=-1, keepdims=True) + jnp.sum(r_hi, axis=-1, keepdims=True)) * inv_d
    d_lo, d_hi = r_lo - mu, r_hi - mu
    var = (jnp.sum(d_lo * d_lo, axis=-1, keepdims=True) + jnp.sum(d_hi * d_hi, axis=-1, keepdims=True)) * inv_d
    rstd = lax.rsqrt(var + LN_EPS)
    g, b = g_ref[...], b_ref[...]
    o_lo = d_lo * rstd * g[:, :half] + b[:, :half]
    o_hi = d_hi * rstd * g[:, half:] + b[:, half:]
    h32_ref[:, :half] = o_lo
    h32_ref[:, half:] = o_hi
    h16_ref[:, :half] = o_lo.astype(bf16)
    h16_ref[:, half:] = o_hi.astype(bf16)


def _combine(dest_tiles, x1, tw_t, ln_g, ln_b, y, alpha):
    m, d = x1.shape
    tm = dest_tiles.shape[2] // 2
    row = lambda i: (i, 0)
    fix = lambda i: (0, 0)
    return pl.pallas_call(
        functools.partial(_combine_kernel, tm=tm, alpha=alpha),
        grid=(m // tm,),
        in_specs=[pl.BlockSpec(memory_space=pl.ANY),
                  pl.BlockSpec((tm, d), row),
                  pl.BlockSpec((tm, 2), row),
                  pl.BlockSpec((1, d), fix), pl.BlockSpec((1, d), fix),
                  pl.BlockSpec(memory_space=pl.ANY)],
        out_specs=[pl.BlockSpec((tm, d), row), pl.BlockSpec((tm, d), row)],
        out_shape=[jax.ShapeDtypeStruct((m, d), f32), jax.ShapeDtypeStruct((m, d), bf16)],
        scratch_shapes=[pltpu.SMEM((1, 2 * tm), i32), pltpu.VMEM((2, tm, d // 2), u32),
                        pltpu.SemaphoreType.DMA(()), pltpu.SemaphoreType.DMA(())],
        compiler_params=_cparams(("arbitrary",)),
    )(dest_tiles, x1, tw_t, ln_g, ln_b, y)


def _moe_plan(top_e, n_tok):
    rank, counts = _expert_ranks(top_e.reshape(1, 2 * n_tok))
    counts = counts.reshape(N_EXPERTS).astype(i32)
    padded = (counts + MOE_BLK - 1) // MOE_BLK * MOE_BLK
    pend = jnp.cumsum(padded)
    pstart = pend - padded
    e_flat = top_e.reshape(2 * n_tok)
    start_of = jnp.sum(jnp.where(e_flat[:, None] == jnp.arange(N_EXPERTS, dtype=i32)[None, :],
                                 pstart[None, :], 0), axis=1)
    dest = (start_of + rank.reshape(2 * n_tok)).reshape(2, n_tok)
    n_blocks = -(-(2 * n_tok) // MOE_BLK) + N_EXPERTS
    blk_row = jnp.arange(n_blocks, dtype=i32) * MOE_BLK
    block_e = jnp.minimum(jnp.sum((blk_row[:, None] >= pend[None, :]).astype(i32), axis=1), N_EXPERTS - 1)
    n_active = (pend[-1] // MOE_BLK).reshape(1).astype(i32)
    tm = _pick(n_tok, (TOK_TILE, 128))
    dest_tiles = dest.reshape(2, n_tok // tm, tm).transpose(1, 0, 2).reshape(n_tok // tm, 1, 2 * tm)
    return dest_tiles, block_e.astype(i32), n_active, n_blocks * MOE_BLK


def kernel(x, meta_tokens, w_in, w_gate_lr, b_gate_lr, sb_out_norm, gla_out_norm, w_out, ln1_g, ln1_b,
           w_router, b_router, w_ff_gate, w_ff_up, w_ff_down, ln2_g, ln2_b):
    bsz, seq, d = x.shape
    depth = w_in.shape[0]
    alpha = float((2 * depth) ** 0.25)
    length = N_META + seq
    lp = -(-length // SEQ_ALIGN) * SEQ_ALIGN
    pad = lp - length
    assert (pad - (-length) % GLA_CHUNK) % GLA_CHUNK == 0
    n_tok = bsz * lp

    meta = jnp.broadcast_to(meta_tokens[None].astype(x.dtype), (bsz, N_META, d))
    h = jnp.concatenate([jnp.zeros((bsz, pad, d), x.dtype), meta, x], axis=1).reshape(n_tok, d)
    h16 = h.astype(bf16)

    sb_w = SB_HEADS * SB_HEAD_DIM
    gla_kw = GLA_HEADS * GLA_DK
    gla_vw = GLA_HEADS * GLA_DV
    o_lr = 3 * sb_w + 2 * gla_kw + gla_vw
    o_go = o_lr + GLA_GATE_RANK
    w_main = jnp.concatenate([w_in[:, :, :o_lr], w_in[:, :, o_go:]], axis=2).astype(bf16)
    w_lr = jnp.pad(w_in[:, :, o_lr:o_go], ((0, 0), (0, 0), (0, LANES - GLA_GATE_RANK))).astype(bf16)
    w_g = jnp.pad(w_gate_lr, ((0, 0), (0, LANES - GLA_GATE_RANK), (0, 0))).astype(bf16)
    w_out16 = w_out.astype(bf16)
    wr_t = w_router.T
    wr_hi = wr_t.astype(bf16)
    wr_lo = (wr_t - wr_hi.astype(f32)).astype(bf16)
    wr = jnp.concatenate([wr_hi, wr_lo], axis=0)
    br = b_router.astype(f32).reshape(N_EXPERTS, 1)
    wg16, wu16, wd16 = w_ff_gate.astype(bf16), w_ff_up.astype(bf16), w_ff_down.astype(bf16)

    for l in range(depth):
        proj = _matmul(h16, w_main[l], bf16)
        log_a = _gla_gate(h16, w_lr[l], w_g[l], b_gate_lr[l].reshape(1, gla_kw))
        proj3 = proj.reshape(bsz, lp, proj.shape[1])
        y_sb = _sb_attention(proj3, sb_out_norm[l].reshape(1, sb_w), pad)
        y_gla = _gla(proj3, log_a.reshape(bsz, lp, gla_kw), gla_out_norm[l].reshape(1, gla_vw), pad)
        x1, x1p, top_e, top_w = _oproj_ln_route(
            y_sb.reshape(n_tok, sb_w), y_gla.reshape(n_tok, gla_vw), w_out16[l, :sb_w], w_out16[l, sb_w:],
            h, ln1_g[l].reshape(1, d), ln1_b[l].reshape(1, d), wr, br, alpha)
        dest_tiles, block_e, n_active, n_rows = _moe_plan(top_e, n_tok)
        xs = _dispatch(x1p, dest_tiles, n_rows)
        y = _expert_ffn(xs, block_e, n_active, wg16[l], wu16[l], wd16[l])
        h, h16 = _combine(dest_tiles, x1, top_w.T, ln2_g[l].reshape(1, d), ln2_b[l].reshape(1, d), y, alpha)

    return h.reshape(bsz, lp, d)[:, pad + N_META:, :]
```

```python
import functools
import math

import jax
import jax.numpy as jnp
from jax import lax
from jax.experimental import pallas as pl
from jax.experimental.pallas import tpu as pltpu

f32 = jnp.float32
bf16 = jnp.bfloat16
u32 = jnp.uint32
i32 = jnp.int32

N_META = 16
SB_HEADS = 8
SB_HEAD_DIM = 128
GLA_HEADS = 4
GLA_DK = 128
GLA_DV = 256
GLA_GATE_RANK = 16
GLA_GATE_TAU = 16.0
GLA_CHUNK = 64
N_EXPERTS = 16
N_GROUPS = 4
EXPERTS_PER_GROUP = 4
LN_EPS = 1e-5
RMS_EPS = 1e-6

LANES = 128
VMEM_LIMIT = 56 * 1024 * 1024

SEQ_ALIGN = 256
SB_TQ = 128
SB_WINDOW = 3
SB_HEADS_PER_STEP = 4
GLA_ROWS = 256
MOE_BLK = 512
TOK_TILE = 256
RANK_CHUNK = 512
SB_DEAD_LOGW = 104.0


def _pick(n, cands):
    for c in cands:
        if n % c == 0:
            return c
    raise ValueError(f"no tile in {cands} divides {n}")


def _cparams(sem, vmem=VMEM_LIMIT):
    return pltpu.CompilerParams(dimension_semantics=sem, vmem_limit_bytes=vmem)


def _dot(a, b):
    return jnp.dot(a, b, preferred_element_type=f32)


def _dot_nt(a, b):
    return lax.dot_general(a, b, (((1,), (1,)), ((), ())), preferred_element_type=f32)


def _dot_tn(a, b):
    return lax.dot_general(a, b, (((0,), (0,)), ((), ())), preferred_element_type=f32)


def _pack_pair(lo, hi):
    lo_b = lax.bitcast_convert_type(lo.astype(bf16).astype(f32), u32)
    hi_b = lax.bitcast_convert_type(hi.astype(bf16).astype(f32), u32)
    return (hi_b & jnp.uint32(0xFFFF0000)) | (lo_b >> 16)


def _unpack_pair(w):
    lo = lax.bitcast_convert_type(w << 16, f32)
    hi = lax.bitcast_convert_type(w & jnp.uint32(0xFFFF0000), f32)
    return lo, hi


def _mm_kernel(x_ref, w_ref, o_ref):
    o_ref[...] = _dot(x_ref[...], w_ref[...]).astype(o_ref.dtype)


def _matmul(x, w, out_dtype):
    m, k = x.shape
    n = w.shape[1]
    tm = _pick(m, (1024, 512, 256, 128))
    tn = _pick(n, (512, 256, 128))
    return pl.pallas_call(
        _mm_kernel,
        grid=(m // tm, n // tn),
        in_specs=[pl.BlockSpec((tm, k), lambda i, j: (i, 0)),
                  pl.BlockSpec((k, tn), lambda i, j: (0, j))],
        out_specs=pl.BlockSpec((tm, tn), lambda i, j: (i, j)),
        out_shape=jax.ShapeDtypeStruct((m, n), out_dtype),
        compiler_params=_cparams(("parallel", "parallel")),
    )(x, w)


def _gate_kernel(x_ref, wlr_ref, wg_ref, b_ref, o_ref):
    g_lr = _dot(x_ref[...], wlr_ref[...]).astype(bf16)
    y = _dot(g_lr, wg_ref[...]) + b_ref[...]
    log_sig = jnp.minimum(y, 0.0) - jnp.log(1.0 + jnp.exp(-jnp.abs(y)))
    o_ref[...] = log_sig * (1.0 / GLA_GATE_TAU)


def _gla_gate(x, w_lr, w_g, b_g):
    m, k = x.shape
    n = w_g.shape[1]
    tm = _pick(m, (512, 256, 128))
    return pl.pallas_call(
        _gate_kernel,
        grid=(m // tm,),
        in_specs=[pl.BlockSpec((tm, k), lambda i: (i, 0)),
                  pl.BlockSpec((k, LANES), lambda i: (0, 0)),
                  pl.BlockSpec((LANES, n), lambda i: (0, 0)),
                  pl.BlockSpec((1, n), lambda i: (0, 0))],
        out_specs=pl.BlockSpec((tm, n), lambda i: (i, 0)),
        out_shape=jax.ShapeDtypeStruct((m, n), f32),
        compiler_params=_cparams(("parallel",)),
    )(x, w_lr, w_g, b_g)


def _sb_blocks(q_ref, k_ref, v_ref, tri, heads, k0, n_blocks, valid, carries):
    tq, d = SB_TQ, SB_HEAD_DIM
    scale = d ** -0.5
    span = pl.ds(k0, n_blocks * tq)
    cols = [slice(h * d, (h + 1) * d) for h in range(heads)]
    z = [_dot_nt(q_ref[0, :, cs], k_ref[0, span, cs]) * scale for cs in cols]
    log_beta, sums = [], []
    for h in range(heads):
        for j in range(n_blocks):
            zj = z[h][:, j * tq:(j + 1) * tq]
            sp = jnp.maximum(zj, 0.0) + jnp.log(1.0 + jnp.exp(-jnp.abs(zj)))
            log_beta.append(zj - sp)
            sums.append(_dot(jnp.where(valid[j], sp, 0.0).astype(bf16), tri))
    outs, new_carries = [], []
    for h in range(heads):
        carry = carries[h]
        w = [None] * n_blocks
        for j in reversed(range(n_blocks)):
            sm = sums[h * n_blocks + j]
            between = sm[:, :tq] + carry
            w[j] = jnp.where(valid[j], jnp.exp(log_beta[h * n_blocks + j] - between), 0.0).astype(bf16)
            carry = carry + sm[:, tq:]
        outs.append(_dot(w[0] if n_blocks == 1 else jnp.concatenate(w, axis=1), v_ref[0, span, cols[h]]))
        new_carries.append(carry)
    return outs, new_carries


def _sb_kernel(q_ref, k_ref, v_ref, tri_ref, g_ref, o_ref, *, pad, heads):
    tq, d = SB_TQ, SB_HEAD_DIM
    i = pl.program_id(2)
    t0 = i * tq
    wb = jnp.maximum(i - (SB_WINDOW - 1), 0)
    ws = pl.multiple_of(wb * tq, tq)
    key_minus_query = lax.broadcasted_iota(i32, (tq, tq), 1) - lax.broadcasted_iota(i32, (tq, tq), 0)
    col = lax.broadcasted_iota(i32, (tq, tq), 1)
    tri = tri_ref[...]

    def valid_for(k0):
        return jnp.logical_and(key_minus_query < t0 - k0, col >= pad - k0)

    def lowest(carries):
        m = jnp.min(carries[0])
        for c in carries[1:]:
            m = jnp.minimum(m, jnp.min(c))
        return m

    zero = [jnp.zeros((tq, tq), f32)] * heads
    accs, carries = _sb_blocks(q_ref, k_ref, v_ref, tri, heads, ws, SB_WINDOW,
                               [valid_for(ws + j * tq) for j in range(SB_WINDOW)], zero)

    def cond(c):
        jb, cmin, _, _ = c
        return jnp.logical_and(jb >= 0, cmin < SB_DEAD_LOGW)

    def body(c):
        jb, _, accs, carries = c
        k0 = pl.multiple_of(jb * tq, tq)
        outs, carries = _sb_blocks(q_ref, k_ref, v_ref, tri, heads, k0, 1, [valid_for(k0)], list(carries))
        return jb - 1, lowest(carries), tuple(a + o for a, o in zip(accs, outs)), tuple(carries)

    _, _, accs, _ = lax.while_loop(cond, body, (wb - 1, lowest(carries), tuple(accs), tuple(carries)))
    for h in range(heads):
        cs = slice(h * d, (h + 1) * d)
        ms = jnp.mean(accs[h] * accs[h], axis=1, keepdims=True)
        o_ref[0, :, cs] = (accs[h] * lax.rsqrt(ms + RMS_EPS) * g_ref[:, cs]).astype(o_ref.dtype)


def _sb_attention(proj, norm_g, pad):
    b, lp, _ = proj.shape
    tq, heads = SB_TQ, SB_HEADS_PER_STEP
    wid = heads * SB_HEAD_DIM
    groups = SB_HEADS // heads
    later = jnp.arange(tq)[:, None] > jnp.arange(tq)[None, :]
    tri = jnp.concatenate([later, jnp.ones((tq, tq), bool)], axis=1).astype(bf16)
    return pl.pallas_call(
        functools.partial(_sb_kernel, pad=pad, heads=heads),
        grid=(b, groups, lp // tq),
        in_specs=[pl.BlockSpec((1, tq, wid), lambda bi, h, i: (bi, i, h)),
                  pl.BlockSpec((1, lp, wid), lambda bi, h, i: (bi, 0, groups + h)),
                  pl.BlockSpec((1, lp, wid), lambda bi, h, i: (bi, 0, 2 * groups + h)),
                  pl.BlockSpec((tq, 2 * tq), lambda bi, h, i: (0, 0)),
                  pl.BlockSpec((1, wid), lambda bi, h, i: (0, h))],
        out_specs=pl.BlockSpec((1, tq, wid), lambda bi, h, i: (bi, i, h)),
        out_shape=jax.ShapeDtypeStruct((b, lp, SB_HEADS * SB_HEAD_DIM), bf16),
        compiler_params=_cparams(("parallel", "parallel", "parallel")),
    )(proj, proj, proj, tri, norm_g)


def _gla_kernel(q_ref, k_ref, v_ref, la_ref, go_ref, gn_ref, o_ref, st_ref, *, pad):
    c, rows, dk, dv = GLA_CHUNK, GLA_ROWS, GLA_DK, GLA_DV
    s = pl.program_id(1)

    @pl.when(s == 0)
    def _():
        st_ref[...] = jnp.zeros_like(st_ref)

    real = (s * rows + lax.broadcasted_iota(i32, (rows, 1), 0)) >= pad
    r_i = lax.broadcasted_iota(i32, (rows, rows), 0)
    c_i = lax.broadcasted_iota(i32, (rows, rows), 1)
    shift = c.bit_length() - 1
    causal = jnp.logical_and(c_i <= r_i, (r_i >> shift) == (c_i >> shift))
    causal_b = causal.astype(bf16)
    scale = dk ** -0.5

    for h in range(GLA_HEADS):
        ks = slice(h * dk, (h + 1) * dk)
        vs = slice(h * dv, (h + 1) * dv)
        g = jnp.where(real, la_ref[0, :, ks], 0.0)
        g_hi = g.astype(bf16)
        g_lo = (g - g_hi.astype(f32)).astype(bf16)
        bcum = _dot(causal_b, g_hi) + _dot(causal_b, g_lo)
        qf = q_ref[0, :, ks].astype(f32) * scale
        kf = jnp.where(real, k_ref[0, :, ks].astype(f32), 0.0)
        vb = v_ref[0, :, vs]
        q_dec = (qf * jnp.exp(bcum)).astype(bf16)
        k_inv = (kf * jnp.exp(-bcum)).astype(bf16)
        scores = jnp.where(causal, _dot_nt(q_dec, k_inv), 0.0)
        o_intra = _dot(scores.astype(bf16), vb)
        st = st_ref[h]
        o_inter = []
        for n in range(rows // c):
            rs = slice(n * c, (n + 1) * c)
            b_last = bcum[(n + 1) * c - 1:(n + 1) * c, :]
            k_dec = (kf[rs] * jnp.exp(b_last - bcum[rs])).astype(bf16)
            o_inter.append(_dot_nt(q_dec[rs], st.astype(bf16)))
            st = st * jnp.exp(b_last) + _dot_tn(vb[rs], k_dec)
        st_ref[h] = st
        o = o_intra + jnp.concatenate(o_inter, axis=0)
        ms = jnp.mean(o * o, axis=1, keepdims=True)
        gate = go_ref[0, :, vs].astype(f32)
        y = o * lax.rsqrt(ms + RMS_EPS) * gn_ref[:, vs] * (gate * jax.nn.sigmoid(gate))
        o_ref[0, :, vs] = y.astype(o_ref.dtype)


def _gla(proj, log_a, norm_g, pad):
    b, lp, _ = proj.shape
    rows = GLA_ROWS
    kw, vw = GLA_HEADS * GLA_DK, GLA_HEADS * GLA_DV
    return pl.pallas_call(
        functools.partial(_gla_kernel, pad=pad),
        grid=(b, lp // rows),
        in_specs=[pl.BlockSpec((1, rows, kw), lambda bi, s: (bi, s, 6)),
                  pl.BlockSpec((1, rows, kw), lambda bi, s: (bi, s, 7)),
                  pl.BlockSpec((1, rows, vw), lambda bi, s: (bi, s, 4)),
                  pl.BlockSpec((1, rows, kw), lambda bi, s: (bi, s, 0)),
                  pl.BlockSpec((1, rows, vw), lambda bi, s: (bi, s, 5)),
                  pl.BlockSpec((1, vw), lambda bi, s: (0, 0))],
        out_specs=pl.BlockSpec((1, rows, vw), lambda bi, s: (bi, s, 0)),
        out_shape=jax.ShapeDtypeStruct((b, lp, vw), bf16),
        scratch_shapes=[pltpu.VMEM((GLA_HEADS, GLA_DV, GLA_DK), f32)],
        compiler_params=_cparams(("parallel", "arbitrary")),
    )(proj, proj, proj, log_a, proj, norm_g)


def _layer_norm_rows(r, g, b):
    mu = jnp.mean(r, axis=-1, keepdims=True)
    d = r - mu
    var = jnp.mean(d * d, axis=-1, keepdims=True)
    return d * lax.rsqrt(var + LN_EPS) * g + b


def _route(logits):
    m = jnp.max(logits, axis=0, keepdims=True)
    e = jnp.exp(logits - m)
    aff = e / jnp.sum(e, axis=0, keepdims=True)
    rows = [aff[j:j + 1, :] for j in range(N_EXPERTS)]
    scores = []
    for g in range(N_GROUPS):
        a, b, c, d = rows[4 * g:4 * g + 4]
        hi1, lo1 = jnp.maximum(a, b), jnp.minimum(a, b)
        hi2, lo2 = jnp.maximum(c, d), jnp.minimum(c, d)
        scores.append(jnp.maximum(hi1, hi2) + jnp.maximum(jnp.minimum(hi1, hi2), jnp.maximum(lo1, lo2)))
    best = scores[0]
    g_sel = jnp.zeros(best.shape, i32)
    for g in range(1, N_GROUPS):
        better = scores[g] > best
        g_sel = jnp.where(better, g, g_sel)
        best = jnp.where(better, scores[g], best)
    sel = list(rows[0:4])
    for g in range(1, N_GROUPS):
        here = g_sel == g
        sel = [jnp.where(here, rows[4 * g + j], sel[j]) for j in range(4)]
    v1, i1 = sel[0], jnp.zeros(best.shape, i32)
    for j in range(1, 4):
        gt = sel[j] > v1
        i1 = jnp.where(gt, j, i1)
        v1 = jnp.where(gt, sel[j], v1)
    v2, i2 = jnp.full(best.shape, -1.0, f32), jnp.zeros(best.shape, i32)
    for j in range(4):
        cand = jnp.where(i1 == j, -1.0, sel[j])
        gt = cand > v2
        i2 = jnp.where(gt, j, i2)
        v2 = jnp.where(gt, cand, v2)
    s = v1 + v2
    top_e = jnp.concatenate([4 * g_sel + i1, 4 * g_sel + i2], axis=0)
    top_w = jnp.concatenate([v1 / s, v2 / s], axis=0)
    return top_e, top_w


def _oproj_kernel(ysb_ref, ygla_ref, w1_ref, w2_ref, h_ref, g_ref, b_ref, wr_ref, br_ref,
                  x1_ref, x1p_ref, te_ref, tw_ref, *, alpha):
    mix = _dot(ysb_ref[...], w1_ref[...]) + _dot(ygla_ref[...], w2_ref[...])
    x1 = _layer_norm_rows(alpha * h_ref[...] + mix, g_ref[...], b_ref[...])
    x1_ref[...] = x1
    half = x1.shape[1] // 2
    x1p_ref[...] = _pack_pair(x1[:, :half], x1[:, half:])
    xh = x1.astype(bf16)
    xl = (x1 - xh.astype(f32)).astype(bf16)
    wr = wr_ref[...]
    l_h = _dot_nt(wr, xh)
    l_l = _dot_nt(wr[:N_EXPERTS], xl)
    logits = l_h[:N_EXPERTS] + l_h[N_EXPERTS:] + l_l + br_ref[...]
    top_e, top_w = _route(logits)
    te_ref[...] = top_e
    tw_ref[...] = top_w


def _oproj_ln_route(y_sb, y_gla, w1, w2, h, ln_g, ln_b, wr, br, alpha):
    m, d = h.shape
    kh = y_sb.shape[1]
    tm = _pick(m, (512, 256, 128))
    row = lambda i: (i, 0)
    fix = lambda i: (0, 0)
    col = lambda i: (0, i)
    return pl.pallas_call(
        functools.partial(_oproj_kernel, alpha=alpha),
        grid=(m // tm,),
        in_specs=[pl.BlockSpec((tm, kh), row), pl.BlockSpec((tm, kh), row),
                  pl.BlockSpec((kh, d), fix), pl.BlockSpec((kh, d), fix),
                  pl.BlockSpec((tm, d), row),
                  pl.BlockSpec((1, d), fix), pl.BlockSpec((1, d), fix),
                  pl.BlockSpec((2 * N_EXPERTS, d), fix), pl.BlockSpec((N_EXPERTS, 1), fix)],
        out_specs=[pl.BlockSpec((tm, d), row), pl.BlockSpec((tm, d // 2), row),
                   pl.BlockSpec((2, tm), col), pl.BlockSpec((2, tm), col)],
        out_shape=[jax.ShapeDtypeStruct((m, d), f32), jax.ShapeDtypeStruct((m, d // 2), u32),
                   jax.ShapeDtypeStruct((2, m), i32), jax.ShapeDtypeStruct((2, m), f32)],
        compiler_params=_cparams(("parallel",)),
    )(y_sb, y_gla, w1, w2, h, ln_g, ln_b, wr, br)


def _rank_kernel(e_ref, rank_ref, cnt_ref, carry_ref, *, chunk):
    @pl.when(pl.program_id(0) == 0)
    def _():
        carry_ref[...] = jnp.zeros_like(carry_ref)

    e = e_ref[...]
    onehot = lax.broadcasted_iota(i32, (N_EXPERTS, chunk), 0) == e
    before = (lax.broadcasted_iota(i32, (chunk, chunk), 0)
              < lax.broadcasted_iota(i32, (chunk, chunk), 1)).astype(bf16)
    seen = _dot(onehot.astype(bf16), before) + carry_ref[...]
    rank_ref[...] = jnp.sum(jnp.where(onehot, seen, 0.0), axis=0, keepdims=True).astype(i32)
    total = carry_ref[...] + jnp.sum(onehot.astype(f32), axis=1, keepdims=True)
    carry_ref[...] = total
    cnt_ref[...] = total


def _expert_ranks(e_flat):
    n = e_flat.shape[1]
    chunk = _pick(n, (RANK_CHUNK, 256, 128))
    return pl.pallas_call(
        functools.partial(_rank_kernel, chunk=chunk),
        grid=(n // chunk,),
        in_specs=[pl.BlockSpec((1, chunk), lambda i: (0, i))],
        out_specs=[pl.BlockSpec((1, chunk), lambda i: (0, i)),
                   pl.BlockSpec((N_EXPERTS, 1), lambda i: (0, 0))],
        out_shape=[jax.ShapeDtypeStruct((1, n), i32), jax.ShapeDtypeStruct((N_EXPERTS, 1), f32)],
        scratch_shapes=[pltpu.VMEM((N_EXPERTS, 1), f32)],
        compiler_params=_cparams(("arbitrary",)),
    )(e_flat)


def _row_copy(src, dst, sem):
    return pltpu.make_async_copy(src, dst, sem)


def _dispatch_kernel(dest_hbm, x_ref, zeros_hbm, xs_hbm, dsm, sem_i, sem_d, *, tm):
    del zeros_hbm
    i = pl.program_id(0)
    idx_cp = _row_copy(dest_hbm.at[i], dsm, sem_i)
    idx_cp.start()
    idx_cp.wait()

    def body(r, carry):
        for k in range(2):
            d = dsm[0, k * tm + r]
            _row_copy(x_ref.at[pl.ds(r, 1)], xs_hbm.at[pl.ds(d, 1)], sem_d).start()
        return carry

    lax.fori_loop(0, tm, body, 0)
    for k in range(2):
        _row_copy(x_ref, xs_hbm.at[pl.ds(0, tm)], sem_d).wait()


def _dispatch(x_packed, dest_tiles, n_rows):
    m, w = x_packed.shape
    tm = dest_tiles.shape[2] // 2
    zeros = jnp.zeros((n_rows, w), u32)
    return pl.pallas_call(
        functools.partial(_dispatch_kernel, tm=tm),
        grid=(m // tm,),
        in_specs=[pl.BlockSpec(memory_space=pl.ANY),
                  pl.BlockSpec((tm, w), lambda i: (i, 0)),
                  pl.BlockSpec(memory_space=pl.ANY)],
        out_specs=pl.BlockSpec(memory_space=pl.ANY),
        out_shape=jax.ShapeDtypeStruct((n_rows, w), u32),
        scratch_shapes=[pltpu.SMEM((1, 2 * tm), i32), pltpu.SemaphoreType.DMA(()), pltpu.SemaphoreType.DMA(())],
        input_output_aliases={2: 0},
        compiler_params=_cparams(("arbitrary",)),
    )(dest_tiles, x_packed, zeros)


def _ffn_kernel(be_ref, na_ref, xs_ref, wg_ref, wu_ref, wd_ref, y_ref):
    del be_ref
    blk = pl.program_id(0)

    @pl.when(blk < na_ref[0])
    def _():
        lo, hi = _unpack_pair(xs_ref[...])
        lo, hi = lo.astype(bf16), hi.astype(bf16)
        half = lo.shape[1]
        gate = _dot(lo, wg_ref[0, :half, :]) + _dot(hi, wg_ref[0, half:, :])
        up = _dot(lo, wu_ref[0, :half, :]) + _dot(hi, wu_ref[0, half:, :])
        hid = (gate * jax.nn.sigmoid(gate) * up).astype(bf16)
        out = _dot(hid, wd_ref[0])
        y_ref[...] = _pack_pair(out[:, :half], out[:, half:])

    @pl.when(blk >= na_ref[0])
    def _():
        y_ref[...] = jnp.zeros_like(y_ref)


def _expert_ffn(xs, block_e, n_active, wg, wu, wd):
    n_rows, w = xs.shape
    d, dff = wg.shape[1], wg.shape[2]
    n_blocks = n_rows // MOE_BLK
    grid_spec = pltpu.PrefetchScalarGridSpec(
        num_scalar_prefetch=2,
        grid=(n_blocks,),
        in_specs=[pl.BlockSpec((MOE_BLK, w), lambda b, be, na: (b, 0)),
                  pl.BlockSpec((1, d, dff), lambda b, be, na: (be[b], 0, 0)),
                  pl.BlockSpec((1, d, dff), lambda b, be, na: (be[b], 0, 0)),
                  pl.BlockSpec((1, dff, d), lambda b, be, na: (be[b], 0, 0))],
        out_specs=pl.BlockSpec((MOE_BLK, w), lambda b, be, na: (b, 0)),
    )
    return pl.pallas_call(
        _ffn_kernel,
        grid_spec=grid_spec,
        out_shape=jax.ShapeDtypeStruct((n_rows, w), u32),
        compiler_params=_cparams(("arbitrary",)),
    )(block_e, n_active, xs, wg, wu, wd)


def _combine_kernel(dest_hbm, x1_ref, tw_ref, g_ref, b_ref, y_hbm, h32_ref, h16_ref,
                    dsm, ybuf, sem_i, sem_g, *, tm, alpha):
    i = pl.program_id(0)
    idx_cp = _row_copy(dest_hbm.at[i], dsm, sem_i)
    idx_cp.start()
    idx_cp.wait()

    def body(r, carry):
        for k in range(2):
            d = dsm[0, k * tm + r]
            _row_copy(y_hbm.at[pl.ds(d, 1)], ybuf.at[k, pl.ds(r, 1)], sem_g).start()
        return carry

    lax.fori_loop(0, tm, body, 0)
    for k in range(2):
        _row_copy(y_hbm.at[pl.ds(0, tm)], ybuf.at[k], sem_g).wait()

    tw = tw_ref[...]
    w0, w1 = tw[:, 0:1], tw[:, 1:2]
    lo0, hi0 = _unpack_pair(ybuf[0])
    lo1, hi1 = _unpack_pair(ybuf[1])
    half = lo0.shape[1]
    x1 = x1_ref[...]
    r_lo = alpha * x1[:, :half] + (w0 * lo0 + w1 * lo1)
    r_hi = alpha * x1[:, half:] + (w0 * hi0 + w1 * hi1)
    inv_d = 1.0 / (2 * half)
    mu = (jnp.sum(r_lo, axis=-1, keepdims=True) + jnp.sum(r_hi, axis=-1, keepdims=True)) * inv_d
    d_lo, d_hi = r_lo - mu, r_hi - mu
    var = (jnp.sum(d_lo * d_lo, axis=-1, keepdims=True) + jnp.sum(d_hi * d_hi, axis=-1, keepdims=True)) * inv_d
    rstd = lax.rsqrt(var + LN_EPS)
    g, b = g_ref[...], b_ref[...]
    o_lo = d_lo * rstd * g[:, :half] + b[:, :half]
    o_hi = d_hi * rstd * g[:, half:] + b[:, half:]
    h32_ref[:, :half] = o_lo
    h32_ref[:, half:] = o_hi
    h16_ref[:, :half] = o_lo.astype(bf16)
    h16_ref[:, half:] = o_hi.astype(bf16)


def _combine(dest_tiles, x1, tw_t, ln_g, ln_b, y, alpha):
    m, d = x1.shape
    tm = dest_tiles.shape[2] // 2
    row = lambda i: (i, 0)
    fix = lambda i: (0, 0)
    return pl.pallas_call(
        functools.partial(_combine_kernel, tm=tm, alpha=alpha),
        grid=(m // tm,),
        in_specs=[pl.BlockSpec(memory_space=pl.ANY),
                  pl.BlockSpec((tm, d), row),
                  pl.BlockSpec((tm, 2), row),
                  pl.BlockSpec((1, d), fix), pl.BlockSpec((1, d), fix),
                  pl.BlockSpec(memory_space=pl.ANY)],
        out_specs=[pl.BlockSpec((tm, d), row), pl.BlockSpec((tm, d), row)],
        out_shape=[jax.ShapeDtypeStruct((m, d), f32), jax.ShapeDtypeStruct((m, d), bf16)],
        scratch_shapes=[pltpu.SMEM((1, 2 * tm), i32), pltpu.VMEM((2, tm, d // 2), u32),
                        pltpu.SemaphoreType.DMA(()), pltpu.SemaphoreType.DMA(())],
        compiler_params=_cparams(("arbitrary",)),
    )(dest_tiles, x1, tw_t, ln_g, ln_b, y)


def _moe_plan(top_e, n_tok):
    rank, counts = _expert_ranks(top_e.reshape(1, 2 * n_tok))
    counts = counts.reshape(N_EXPERTS).astype(i32)
    padded = (counts + MOE_BLK - 1) // MOE_BLK * MOE_BLK
    pend = jnp.cumsum(padded)
    pstart = pend - padded
    e_flat = top_e.reshape(2 * n_tok)
    start_of = jnp.sum(jnp.where(e_flat[:, None] == jnp.arange(N_EXPERTS, dtype=i32)[None, :],
                                 pstart[None, :], 0), axis=1)
    dest = (start_of + rank.reshape(2 * n_tok)).reshape(2, n_tok)
    n_blocks = -(-(2 * n_tok) // MOE_BLK) + N_EXPERTS
    blk_row = jnp.arange(n_blocks, dtype=i32) * MOE_BLK
    block_e = jnp.minimum(jnp.sum((blk_row[:, None] >= pend[None, :]).astype(i32), axis=1), N_EXPERTS - 1)
    n_active = (pend[-1] // MOE_BLK).reshape(1).astype(i32)
    tm = _pick(n_tok, (TOK_TILE, 128))
    dest_tiles = dest.reshape(2, n_tok // tm, tm).transpose(1, 0, 2).reshape(n_tok // tm, 1, 2 * tm)
    return dest_tiles, block_e.astype(i32), n_active, n_blocks * MOE_BLK


def kernel(x, meta_tokens, w_in, w_gate_lr, b_gate_lr, sb_out_norm, gla_out_norm, w_out, ln1_g, ln1_b,
           w_router, b_router, w_ff_gate, w_ff_up, w_ff_down, ln2_g, ln2_b):
    bsz, seq, d = x.shape
    depth = w_in.shape[0]
    alpha = float((2 * depth) ** 0.25)
    length = N_META + seq
    lp = -(-length // SEQ_ALIGN) * SEQ_ALIGN
    pad = lp - length
    assert (pad - (-length) % GLA_CHUNK) % GLA_CHUNK == 0
    n_tok = bsz * lp

    meta = jnp.broadcast_to(meta_tokens[None].astype(x.dtype), (bsz, N_META, d))
    h = jnp.concatenate([jnp.zeros((bsz, pad, d), x.dtype), meta, x], axis=1).reshape(n_tok, d)
    h16 = h.astype(bf16)

    sb_w = SB_HEADS * SB_HEAD_DIM
    gla_kw = GLA_HEADS * GLA_DK
    gla_vw = GLA_HEADS * GLA_DV
    o_lr = 3 * sb_w + 2 * gla_kw + gla_vw
    o_go = o_lr + GLA_GATE_RANK
    w_main = jnp.concatenate([w_in[:, :, :o_lr], w_in[:, :, o_go:]], axis=2).astype(bf16)
    w_lr = jnp.pad(w_in[:, :, o_lr:o_go], ((0, 0), (0, 0), (0, LANES - GLA_GATE_RANK))).astype(bf16)
    w_g = jnp.pad(w_gate_lr, ((0, 0), (0, LANES - GLA_GATE_RANK), (0, 0))).astype(bf16)
    w_out16 = w_out.astype(bf16)
    wr_t = w_router.T
    wr_hi = wr_t.astype(bf16)
    wr_lo = (wr_t - wr_hi.astype(f32)).astype(bf16)
    wr = jnp.concatenate([wr_hi, wr_lo], axis=0)
    br = b_router.astype(f32).reshape(N_EXPERTS, 1)
    wg16, wu16, wd16 = w_ff_gate.astype(bf16), w_ff_up.astype(bf16), w_ff_down.astype(bf16)

    for l in range(depth):
        proj = _matmul(h16, w_main[l], bf16)
        log_a = _gla_gate(h16, w_lr[l], w_g[l], b_gate_lr[l].reshape(1, gla_kw))
        proj3 = proj.reshape(bsz, lp, proj.shape[1])
        y_sb = _sb_attention(proj3, sb_out_norm[l].reshape(1, sb_w), pad)
        y_gla = _gla(proj3, log_a.reshape(bsz, lp, gla_kw), gla_out_norm[l].reshape(1, gla_vw), pad)
        x1, x1p, top_e, top_w = _oproj_ln_route(
            y_sb.reshape(n_tok, sb_w), y_gla.reshape(n_tok, gla_vw), w_out16[l, :sb_w], w_out16[l, sb_w:],
            h, ln1_g[l].reshape(1, d), ln1_b[l].reshape(1, d), wr, br, alpha)
        dest_tiles, block_e, n_active, n_rows = _moe_plan(top_e, n_tok)
        xs = _dispatch(x1p, dest_tiles, n_rows)
        y = _expert_ffn(xs, block_e, n_active, wg16[l], wu16[l], wd16[l])
        h, h16 = _combine(dest_tiles, x1, top_w.T, ln2_g[l].reshape(1, d), ln2_b[l].reshape(1, d), y, alpha)

    return h.reshape(bsz, lp, d)[:, pad + N_META:, :]
```

```python
import functools
import math

import jax
import jax.numpy as jnp
from jax import lax
from jax.experimental import pallas as pl
from jax.experimental.pallas import tpu as pltpu

f32 = jnp.float32
bf16 = jnp.bfloat16
u32 = jnp.uint32
i32 = jnp.int32

N_META = 16
SB_HEADS = 8
SB_HEAD_DIM = 128
GLA_HEADS = 4
GLA_DK = 128
GLA_DV = 256
GLA_GATE_RANK = 16
GLA_GATE_TAU = 16.0
GLA_CHUNK = 64
N_EXPERTS = 16
N_GROUPS = 4
EXPERTS_PER_GROUP = 4
LN_EPS = 1e-5
RMS_EPS = 1e-6

LANES = 128
VMEM_LIMIT = 56 * 1024 * 1024

SEQ_ALIGN = 256
SB_TQ = 128
SB_WINDOW = 3
SB_HEADS_PER_STEP = 4
GLA_ROWS = 256
OPROJ_ROWS = 128
MOE_BLK = 512
TOK_TILE = 256
RANK_CHUNK = 512
SB_DEAD_LOGW = 104.0


def _pick(n, cands):
    for c in cands:
        if n % c == 0:
            return c
    raise ValueError(f"no tile in {cands} divides {n}")


def _cparams(sem, vmem=VMEM_LIMIT):
    return pltpu.CompilerParams(dimension_semantics=sem, vmem_limit_bytes=vmem)


def _dot(a, b):
    return jnp.dot(a, b, preferred_element_type=f32)


def _dot_nt(a, b):
    return lax.dot_general(a, b, (((1,), (1,)), ((), ())), preferred_element_type=f32)


def _dot_tn(a, b):
    return lax.dot_general(a, b, (((0,), (0,)), ((), ())), preferred_element_type=f32)


def _pack_pair(lo, hi):
    lo_b = lax.bitcast_convert_type(lo.astype(bf16).astype(f32), u32)
    hi_b = lax.bitcast_convert_type(hi.astype(bf16).astype(f32), u32)
    return (hi_b & jnp.uint32(0xFFFF0000)) | (lo_b >> 16)


def _unpack_pair(w):
    lo = lax.bitcast_convert_type(w << 16, f32)
    hi = lax.bitcast_convert_type(w & jnp.uint32(0xFFFF0000), f32)
    return lo, hi


ROW_TILE = 8


def _store_token_rows(ref, words, t0=0):
    n = words.shape[0]
    for s in range(ROW_TILE):
        ref[pl.ds(t0 * ROW_TILE + s, n, stride=ROW_TILE), :] = words[:, s * LANES:(s + 1) * LANES]


def _load_token_rows(ref, n):
    return [ref[pl.ds(s, n, stride=ROW_TILE), :] for s in range(ROW_TILE)]


def _mm_kernel(x_ref, w_ref, o_ref):
    o_ref[...] = _dot(x_ref[...], w_ref[...]).astype(o_ref.dtype)


def _matmul(x, w, out_dtype):
    m, k = x.shape
    n = w.shape[1]
    tm = _pick(m, (2048, 1024, 512, 256, 128))
    tn = _pick(n, (512, 256, 128))
    return pl.pallas_call(
        _mm_kernel,
        grid=(m // tm, n // tn),
        in_specs=[pl.BlockSpec((tm, k), lambda i, j: (i, 0)),
                  pl.BlockSpec((k, tn), lambda i, j: (0, j))],
        out_specs=pl.BlockSpec((tm, tn), lambda i, j: (i, j)),
        out_shape=jax.ShapeDtypeStruct((m, n), out_dtype),
        compiler_params=_cparams(("parallel", "parallel")),
    )(x, w)


def _gate_kernel(x_ref, wlr_ref, wg_ref, b_ref, o_ref):
    g_lr = _dot(x_ref[...], wlr_ref[...]).astype(bf16)
    y = _dot(g_lr, wg_ref[...]) + b_ref[...]
    log_sig = jnp.minimum(y, 0.0) - jnp.log(1.0 + jnp.exp(-jnp.abs(y)))
    o_ref[...] = log_sig * (1.0 / GLA_GATE_TAU)


def _gla_gate(x, w_lr, w_g, b_g):
    m, k = x.shape
    n = w_g.shape[1]
    tm = _pick(m, (512, 256, 128))
    return pl.pallas_call(
        _gate_kernel,
        grid=(m // tm,),
        in_specs=[pl.BlockSpec((tm, k), lambda i: (i, 0)),
                  pl.BlockSpec((k, LANES), lambda i: (0, 0)),
                  pl.BlockSpec((LANES, n), lambda i: (0, 0)),
                  pl.BlockSpec((1, n), lambda i: (0, 0))],
        out_specs=pl.BlockSpec((tm, n), lambda i: (i, 0)),
        out_shape=jax.ShapeDtypeStruct((m, n), f32),
        compiler_params=_cparams(("parallel",)),
    )(x, w_lr, w_g, b_g)


def _sb_blocks(q_ref, k_ref, v_ref, tri, heads, k0, n_blocks, valid, carries):
    tq, d = SB_TQ, SB_HEAD_DIM
    scale = d ** -0.5
    span = pl.ds(k0, n_blocks * tq)
    cols = [slice(h * d, (h + 1) * d) for h in range(heads)]
    z = [_dot_nt(q_ref[0, :, cs], k_ref[0, span, cs]) * scale for cs in cols]
    log_beta, sums = [], []
    for h in range(heads):
        for j in range(n_blocks):
            zj = z[h][:, j * tq:(j + 1) * tq]
            sp = jnp.maximum(zj, 0.0) + jnp.log(1.0 + jnp.exp(-jnp.abs(zj)))
            log_beta.append(zj - sp)
            sums.append(_dot(jnp.where(valid[j], sp, 0.0).astype(bf16), tri))
    outs, new_carries = [], []
    for h in range(heads):
        carry = carries[h]
        w = [None] * n_blocks
        for j in reversed(range(n_blocks)):
            sm = sums[h * n_blocks + j]
            between = sm[:, :tq] + carry
            w[j] = jnp.where(valid[j], jnp.exp(log_beta[h * n_blocks + j] - between), 0.0).astype(bf16)
            carry = carry + sm[:, tq:]
        outs.append(_dot(w[0] if n_blocks == 1 else jnp.concatenate(w, axis=1), v_ref[0, span, cols[h]]))
        new_carries.append(carry)
    return outs, new_carries


def _sb_kernel(q_ref, k_ref, v_ref, tri_ref, g_ref, o_ref, *, pad, heads):
    tq, d = SB_TQ, SB_HEAD_DIM
    i = pl.program_id(2)
    t0 = i * tq
    wb = jnp.maximum(i - (SB_WINDOW - 1), 0)
    ws = pl.multiple_of(wb * tq, tq)
    key_minus_query = lax.broadcasted_iota(i32, (tq, tq), 1) - lax.broadcasted_iota(i32, (tq, tq), 0)
    col = lax.broadcasted_iota(i32, (tq, tq), 1)
    tri = tri_ref[...]

    def valid_for(k0):
        return jnp.logical_and(key_minus_query < t0 - k0, col >= pad - k0)

    def lowest(carries):
        m = jnp.min(carries[0])
        for c in carries[1:]:
            m = jnp.minimum(m, jnp.min(c))
        return m

    zero = [jnp.zeros((tq, tq), f32)] * heads
    accs, carries = _sb_blocks(q_ref, k_ref, v_ref, tri, heads, ws, SB_WINDOW,
                               [valid_for(ws + j * tq) for j in range(SB_WINDOW)], zero)

    def cond(c):
        jb, cmin, _, _ = c
        return jnp.logical_and(jb >= 0, cmin < SB_DEAD_LOGW)

    def body(c):
        jb, _, accs, carries = c
        k0 = pl.multiple_of(jb * tq, tq)
        outs, carries = _sb_blocks(q_ref, k_ref, v_ref, tri, heads, k0, 1, [valid_for(k0)], list(carries))
        return jb - 1, lowest(carries), tuple(a + o for a, o in zip(accs, outs)), tuple(carries)

    _, _, accs, _ = lax.while_loop(cond, body, (wb - 1, lowest(carries), tuple(accs), tuple(carries)))
    for h in range(heads):
        cs = slice(h * d, (h + 1) * d)
        ms = jnp.mean(accs[h] * accs[h], axis=1, keepdims=True)
        o_ref[0, :, cs] = (accs[h] * lax.rsqrt(ms + RMS_EPS) * g_ref[:, cs]).astype(o_ref.dtype)


def _sb_attention(proj, norm_g, pad):
    b, lp, _ = proj.shape
    tq, heads = SB_TQ, SB_HEADS_PER_STEP
    wid = heads * SB_HEAD_DIM
    groups = SB_HEADS // heads
    later = jnp.arange(tq)[:, None] > jnp.arange(tq)[None, :]
    tri = jnp.concatenate([later, jnp.ones((tq, tq), bool)], axis=1).astype(bf16)
    return pl.pallas_call(
        functools.partial(_sb_kernel, pad=pad, heads=heads),
        grid=(b, groups, lp // tq),
        in_specs=[pl.BlockSpec((1, tq, wid), lambda bi, h, i: (bi, i, h)),
                  pl.BlockSpec((1, lp, wid), lambda bi, h, i: (bi, 0, groups + h)),
                  pl.BlockSpec((1, lp, wid), lambda bi, h, i: (bi, 0, 2 * groups + h)),
                  pl.BlockSpec((tq, 2 * tq), lambda bi, h, i: (0, 0)),
                  pl.BlockSpec((1, wid), lambda bi, h, i: (0, h))],
        out_specs=pl.BlockSpec((1, tq, wid), lambda bi, h, i: (bi, i, h)),
        out_shape=jax.ShapeDtypeStruct((b, lp, SB_HEADS * SB_HEAD_DIM), bf16),
        compiler_params=_cparams(("parallel", "parallel", "parallel")),
    )(proj, proj, proj, tri, norm_g)


def _gla_kernel(q_ref, k_ref, v_ref, la_ref, go_ref, gn_ref, o_ref, st_ref, *, pad):
    c, rows, dk, dv = GLA_CHUNK, GLA_ROWS, GLA_DK, GLA_DV
    s = pl.program_id(1)

    @pl.when(s == 0)
    def _():
        st_ref[...] = jnp.zeros_like(st_ref)

    real = (s * rows + lax.broadcasted_iota(i32, (rows, 1), 0)) >= pad
    r_i = lax.broadcasted_iota(i32, (rows, rows), 0)
    c_i = lax.broadcasted_iota(i32, (rows, rows), 1)
    shift = c.bit_length() - 1
    causal = jnp.logical_and(c_i <= r_i, (r_i >> shift) == (c_i >> shift))
    causal_b = causal.astype(bf16)
    scale = dk ** -0.5

    for h in range(GLA_HEADS):
        ks = slice(h * dk, (h + 1) * dk)
        vs = slice(h * dv, (h + 1) * dv)
        g = jnp.where(real, la_ref[0, :, ks], 0.0)
        g_hi = g.astype(bf16)
        g_lo = (g - g_hi.astype(f32)).astype(bf16)
        bcum = _dot(causal_b, g_hi) + _dot(causal_b, g_lo)
        qf = q_ref[0, :, ks].astype(f32) * scale
        kf = jnp.where(real, k_ref[0, :, ks].astype(f32), 0.0)
        vb = v_ref[0, :, vs]
        q_dec = (qf * jnp.exp(bcum)).astype(bf16)
        k_inv = (kf * jnp.exp(-bcum)).astype(bf16)
        scores = jnp.where(causal, _dot_nt(q_dec, k_inv), 0.0)
        o_intra = _dot(scores.astype(bf16), vb)
        st = st_ref[h]
        o_inter = []
        for n in range(rows // c):
            rs = slice(n * c, (n + 1) * c)
            b_last = bcum[(n + 1) * c - 1:(n + 1) * c, :]
            k_dec = (kf[rs] * jnp.exp(b_last - bcum[rs])).astype(bf16)
            o_inter.append(_dot_nt(q_dec[rs], st.astype(bf16)))
            st = st * jnp.exp(b_last) + _dot_tn(vb[rs], k_dec)
        st_ref[h] = st
        o = o_intra + jnp.concatenate(o_inter, axis=0)
        ms = jnp.mean(o * o, axis=1, keepdims=True)
        gate = go_ref[0, :, vs].astype(f32)
        y = o * lax.rsqrt(ms + RMS_EPS) * gn_ref[:, vs] * (gate * jax.nn.sigmoid(gate))
        o_ref[0, :, vs] = y.astype(o_ref.dtype)


def _gla(proj, log_a, norm_g, pad):
    b, lp, _ = proj.shape
    rows = GLA_ROWS
    kw, vw = GLA_HEADS * GLA_DK, GLA_HEADS * GLA_DV
    return pl.pallas_call(
        functools.partial(_gla_kernel, pad=pad),
        grid=(b, lp // rows),
        in_specs=[pl.BlockSpec((1, rows, kw), lambda bi, s: (bi, s, 6)),
                  pl.BlockSpec((1, rows, kw), lambda bi, s: (bi, s, 7)),
                  pl.BlockSpec((1, rows, vw), lambda bi, s: (bi, s, 4)),
                  pl.BlockSpec((1, rows, kw), lambda bi, s: (bi, s, 0)),
                  pl.BlockSpec((1, rows, vw), lambda bi, s: (bi, s, 5)),
                  pl.BlockSpec((1, vw), lambda bi, s: (0, 0))],
        out_specs=pl.BlockSpec((1, rows, vw), lambda bi, s: (bi, s, 0)),
        out_shape=jax.ShapeDtypeStruct((b, lp, vw), bf16),
        scratch_shapes=[pltpu.VMEM((GLA_HEADS, GLA_DV, GLA_DK), f32)],
        compiler_params=_cparams(("parallel", "arbitrary")),
    )(proj, proj, proj, log_a, proj, norm_g)


def _layer_norm_rows(r, g, b):
    mu = jnp.mean(r, axis=-1, keepdims=True)
    d = r - mu
    var = jnp.mean(d * d, axis=-1, keepdims=True)
    return d * lax.rsqrt(var + LN_EPS) * g + b


def _route(logits):
    m = jnp.max(logits, axis=0, keepdims=True)
    e = jnp.exp(logits - m)
    aff = e / jnp.sum(e, axis=0, keepdims=True)
    rows = [aff[j:j + 1, :] for j in range(N_EXPERTS)]
    scores = []
    for g in range(N_GROUPS):
        a, b, c, d = rows[4 * g:4 * g + 4]
        hi1, lo1 = jnp.maximum(a, b), jnp.minimum(a, b)
        hi2, lo2 = jnp.maximum(c, d), jnp.minimum(c, d)
        scores.append(jnp.maximum(hi1, hi2) + jnp.maximum(jnp.minimum(hi1, hi2), jnp.maximum(lo1, lo2)))
    best = scores[0]
    g_sel = jnp.zeros(best.shape, i32)
    for g in range(1, N_GROUPS):
        better = scores[g] > best
        g_sel = jnp.where(better, g, g_sel)
        best = jnp.where(better, scores[g], best)
    sel = list(rows[0:4])
    for g in range(1, N_GROUPS):
        here = g_sel == g
        sel = [jnp.where(here, rows[4 * g + j], sel[j]) for j in range(4)]
    v1, i1 = sel[0], jnp.zeros(best.shape, i32)
    for j in range(1, 4):
        gt = sel[j] > v1
        i1 = jnp.where(gt, j, i1)
        v1 = jnp.where(gt, sel[j], v1)
    v2, i2 = jnp.full(best.shape, -1.0, f32), jnp.zeros(best.shape, i32)
    for j in range(4):
        cand = jnp.where(i1 == j, -1.0, sel[j])
        gt = cand > v2
        i2 = jnp.where(gt, j, i2)
        v2 = jnp.where(gt, cand, v2)
    s = v1 + v2
    top_e = jnp.concatenate([4 * g_sel + i1, 4 * g_sel + i2], axis=0)
    top_w = jnp.concatenate([v1 / s, v2 / s], axis=0)
    return top_e, top_w


def _oproj_kernel(ysb_ref, ygla_ref, w1_ref, w2_ref, h_ref, g_ref, b_ref, wr_ref, br_ref,
                  x1_ref, x1p_ref, te_ref, tw_ref, *, alpha):
    tm, d = h_ref.shape
    half = d // 2
    n = tm // OPROJ_ROWS
    groups = [slice(p * OPROJ_ROWS, (p + 1) * OPROJ_ROWS) for p in range(n)]
    mixes = [_dot(ysb_ref[rs, :], w1_ref[...]) + _dot(ygla_ref[rs, :], w2_ref[...]) for rs in groups]
    wr = wr_ref[...]
    for p, rs in enumerate(groups):
        x1 = _layer_norm_rows(alpha * h_ref[rs, :] + mixes[p], g_ref[...], b_ref[...])
        x1_ref[rs, :] = x1
        _store_token_rows(x1p_ref, _pack_pair(x1[:, :half], x1[:, half:]), p * OPROJ_ROWS)
        xh = x1.astype(bf16)
        xl = (x1 - xh.astype(f32)).astype(bf16)
        l_t = (_dot(xh, wr[:d]) + _dot(xl, wr[d:])).T
        logits = l_t[:N_EXPERTS] + l_t[N_EXPERTS:2 * N_EXPERTS] + br_ref[...]
        top_e, top_w = _route(logits)
        te_ref[:, rs] = top_e
        tw_ref[:, rs] = top_w


def _oproj_ln_route(y_sb, y_gla, w1, w2, h, ln_g, ln_b, wr, br, alpha):
    m, d = h.shape
    kh = y_sb.shape[1]
    tm = _pick(m, (512, 256, 128))
    row = lambda i: (i, 0)
    fix = lambda i: (0, 0)
    col = lambda i: (0, i)
    return pl.pallas_call(
        functools.partial(_oproj_kernel, alpha=alpha),
        grid=(m // tm,),
        in_specs=[pl.BlockSpec((tm, kh), row), pl.BlockSpec((tm, kh), row),
                  pl.BlockSpec((kh, d), fix), pl.BlockSpec((kh, d), fix),
                  pl.BlockSpec((tm, d), row),
                  pl.BlockSpec((1, d), fix), pl.BlockSpec((1, d), fix),
                  pl.BlockSpec((2 * d, LANES), fix), pl.BlockSpec((N_EXPERTS, 1), fix)],
        out_specs=[pl.BlockSpec((tm, d), row), pl.BlockSpec((tm * ROW_TILE, LANES), row),
                   pl.BlockSpec((2, tm), col), pl.BlockSpec((2, tm), col)],
        out_shape=[jax.ShapeDtypeStruct((m, d), f32), jax.ShapeDtypeStruct((m * ROW_TILE, LANES), u32),
                   jax.ShapeDtypeStruct((2, m), i32), jax.ShapeDtypeStruct((2, m), f32)],
        compiler_params=_cparams(("parallel",)),
    )(y_sb, y_gla, w1, w2, h, ln_g, ln_b, wr, br)


def _rank_kernel(e_ref, rank_ref, cnt_ref, carry_ref, *, chunk):
    @pl.when(pl.program_id(0) == 0)
    def _():
        carry_ref[...] = jnp.zeros_like(carry_ref)

    e = e_ref[...]
    onehot = lax.broadcasted_iota(i32, (N_EXPERTS, chunk), 0) == e
    before = (lax.broadcasted_iota(i32, (chunk, chunk), 0)
              < lax.broadcasted_iota(i32, (chunk, chunk), 1)).astype(bf16)
    seen = _dot(onehot.astype(bf16), before) + carry_ref[...]
    rank_ref[...] = jnp.sum(jnp.where(onehot, seen, 0.0), axis=0, keepdims=True).astype(i32)
    total = carry_ref[...] + jnp.sum(onehot.astype(f32), axis=1, keepdims=True)
    carry_ref[...] = total
    cnt_ref[...] = total


def _expert_ranks(e_flat):
    n = e_flat.shape[1]
    chunk = _pick(n, (RANK_CHUNK, 256, 128))
    return pl.pallas_call(
        functools.partial(_rank_kernel, chunk=chunk),
        grid=(n // chunk,),
        in_specs=[pl.BlockSpec((1, chunk), lambda i: (0, i))],
        out_specs=[pl.BlockSpec((1, chunk), lambda i: (0, i)),
                   pl.BlockSpec((N_EXPERTS, 1), lambda i: (0, 0))],
        out_shape=[jax.ShapeDtypeStruct((1, n), i32), jax.ShapeDtypeStruct((N_EXPERTS, 1), f32)],
        scratch_shapes=[pltpu.VMEM((N_EXPERTS, 1), f32)],
        compiler_params=_cparams(("arbitrary",)),
    )(e_flat)


def _row_copy(src, dst, sem):
    return pltpu.make_async_copy(src, dst, sem)


def _token_rows(ref, t, n=1):
    start = t * ROW_TILE
    if not isinstance(start, int):
        start = pl.multiple_of(start, ROW_TILE)
    return ref.at[pl.ds(start, n * ROW_TILE)]


def _dispatch_kernel(dest_hbm, x_hbm, zeros_hbm, xs_hbm, dsm, sem_i, sem_d, *, tm, n_tiles):
    del zeros_hbm
    i = pl.program_id(0)
    slot = i % 2

    def idx_copy(t, sl):
        return _row_copy(dest_hbm.at[t], dsm.at[pl.ds(sl, 1)], sem_i.at[sl])

    def wait_tile(sl):
        for _ in range(2):
            _row_copy(_token_rows(x_hbm, 0, tm), _token_rows(xs_hbm, 0, tm), sem_d.at[sl]).wait()

    @pl.when(i == 0)
    def _():
        idx_copy(0, 0).start()

    idx_copy(i, slot).wait()

    @pl.when(i + 1 < n_tiles)
    def _():
        idx_copy(i + 1, 1 - slot).start()

    base = i * tm

    def body(r, carry):
        for k in range(2):
            d = dsm[slot, k * tm + r]
            _row_copy(_token_rows(x_hbm, base + r), _token_rows(xs_hbm, d), sem_d.at[slot]).start()
        return carry

    lax.fori_loop(0, tm, body, 0, unroll=8)

    @pl.when(i > 0)
    def _():
        wait_tile(1 - slot)

    @pl.when(i == n_tiles - 1)
    def _():
        wait_tile(slot)


def _dispatch(x_rows, dest_tiles, n_slots):
    n_tiles = dest_tiles.shape[0]
    tm = dest_tiles.shape[2] // 2
    zeros = jnp.zeros((n_slots * ROW_TILE, LANES), u32)
    return pl.pallas_call(
        functools.partial(_dispatch_kernel, tm=tm, n_tiles=n_tiles),
        grid=(n_tiles,),
        in_specs=[pl.BlockSpec(memory_space=pl.ANY), pl.BlockSpec(memory_space=pl.ANY),
                  pl.BlockSpec(memory_space=pl.ANY)],
        out_specs=pl.BlockSpec(memory_space=pl.ANY),
        out_shape=jax.ShapeDtypeStruct(zeros.shape, u32),
        scratch_shapes=[pltpu.SMEM((2, 2 * tm), i32), pltpu.SemaphoreType.DMA((2,)), pltpu.SemaphoreType.DMA((2,))],
        input_output_aliases={2: 0},
        compiler_params=_cparams(("arbitrary",)),
    )(dest_tiles, x_rows, zeros)


def _ffn_kernel(be_ref, na_ref, xs_ref, wg_ref, wu_ref, wd_ref, y_ref):
    del be_ref
    blk = pl.program_id(0)

    @pl.when(blk < na_ref[0])
    def _():
        parts = [_unpack_pair(w) for w in _load_token_rows(xs_ref, MOE_BLK)]
        lo = jnp.concatenate([p[0].astype(bf16) for p in parts], axis=1)
        hi = jnp.concatenate([p[1].astype(bf16) for p in parts], axis=1)
        half = lo.shape[1]
        gate = _dot(lo, wg_ref[0, :half, :]) + _dot(hi, wg_ref[0, half:, :])
        up = _dot(lo, wu_ref[0, :half, :]) + _dot(hi, wu_ref[0, half:, :])
        hid = (gate * jax.nn.sigmoid(gate) * up).astype(bf16)
        out = _dot(hid, wd_ref[0])
        _store_token_rows(y_ref, _pack_pair(out[:, :half], out[:, half:]))

    @pl.when(blk >= na_ref[0])
    def _():
        y_ref[...] = jnp.zeros_like(y_ref)


def _expert_ffn(xs, block_e, n_active, wg, wu, wd):
    d, dff = wg.shape[1], wg.shape[2]
    n_blocks = xs.shape[0] // (MOE_BLK * ROW_TILE)
    grid_spec = pltpu.PrefetchScalarGridSpec(
        num_scalar_prefetch=2,
        grid=(n_blocks,),
        in_specs=[pl.BlockSpec((MOE_BLK * ROW_TILE, LANES), lambda b, be, na: (b, 0)),
                  pl.BlockSpec((1, d, dff), lambda b, be, na: (be[b], 0, 0)),
                  pl.BlockSpec((1, d, dff), lambda b, be, na: (be[b], 0, 0)),
                  pl.BlockSpec((1, dff, d), lambda b, be, na: (be[b], 0, 0))],
        out_specs=pl.BlockSpec((MOE_BLK * ROW_TILE, LANES), lambda b, be, na: (b, 0)),
    )
    return pl.pallas_call(
        _ffn_kernel,
        grid_spec=grid_spec,
        out_shape=jax.ShapeDtypeStruct(xs.shape, u32),
        compiler_params=_cparams(("arbitrary",)),
    )(block_e, n_active, xs, wg, wu, wd)


def _combine_kernel(dest_hbm, x1_ref, tw_ref, g_ref, b_ref, y_hbm, h32_ref, h16_ref,
                    dsm, ybuf, sem_i, sem_g, *, tm, n_tiles, alpha):
    i = pl.program_id(0)
    slot = i % 2

    def idx_copy(t, sl):
        return _row_copy(dest_hbm.at[t], dsm.at[pl.ds(sl, 1)], sem_i.at[sl])

    def gather(sl):
        def body(r, carry):
            for k in range(2):
                d = dsm[sl, k * tm + r]
                _row_copy(_token_rows(y_hbm, d), _token_rows(ybuf.at[sl, k], r), sem_g.at[sl]).start()
            return carry

        lax.fori_loop(0, tm, body, 0, unroll=8)

    @pl.when(i == 0)
    def _():
        idx_copy(0, 0).start()
        idx_copy(0, 0).wait()
        gather(0)
        if n_tiles > 1:
            idx_copy(1, 1).start()

    @pl.when(i + 1 < n_tiles)
    def _():
        idx_copy(i + 1, 1 - slot).wait()
        gather(1 - slot)

    @pl.when(i + 2 < n_tiles)
    def _():
        idx_copy(i + 2, slot).start()

    for k in range(2):
        _row_copy(_token_rows(y_hbm, 0, tm), ybuf.at[slot, k], sem_g.at[slot]).wait()

    tw = tw_ref[...]
    w0, w1 = tw[:, 0:1], tw[:, 1:2]
    rows0 = _load_token_rows(ybuf.at[slot, 0], tm)
    rows1 = _load_token_rows(ybuf.at[slot, 1], tm)
    half = ROW_TILE * LANES
    res = [None] * (2 * ROW_TILE)
    for s in range(ROW_TILE):
        lo0, hi0 = _unpack_pair(rows0[s])
        lo1, hi1 = _unpack_pair(rows1[s])
        res[s] = alpha * x1_ref[:, s * LANES:(s + 1) * LANES] + (w0 * lo0 + w1 * lo1)
        res[ROW_TILE + s] = alpha * x1_ref[:, half + s * LANES:half + (s + 1) * LANES] + (w0 * hi0 + w1 * hi1)
    inv_d = 1.0 / (2 * half)
    mu = jnp.sum(functools.reduce(jnp.add, res), axis=-1, keepdims=True) * inv_d
    res = [r - mu for r in res]
    var = jnp.sum(functools.reduce(jnp.add, [r * r for r in res]), axis=-1, keepdims=True) * inv_d
    rstd = lax.rsqrt(var + LN_EPS)
    for c, r in enumerate(res):
        cs = slice(c * LANES, (c + 1) * LANES)
        o = r * rstd * g_ref[:, cs] + b_ref[:, cs]
        h32_ref[:, cs] = o
        h16_ref[:, cs] = o.astype(bf16)


def _combine(dest_tiles, x1, tw_t, ln_g, ln_b, y, alpha):
    m, d = x1.shape
    n_tiles = dest_tiles.shape[0]
    tm = dest_tiles.shape[2] // 2
    row = lambda i: (i, 0)
    fix = lambda i: (0, 0)
    return pl.pallas_call(
        functools.partial(_combine_kernel, tm=tm, n_tiles=n_tiles, alpha=alpha),
        grid=(n_tiles,),
        in_specs=[pl.BlockSpec(memory_space=pl.ANY),
                  pl.BlockSpec((tm, d), row),
                  pl.BlockSpec((tm, 2), row),
                  pl.BlockSpec((1, d), fix), pl.BlockSpec((1, d), fix),
                  pl.BlockSpec(memory_space=pl.ANY)],
        out_specs=[pl.BlockSpec((tm, d), row), pl.BlockSpec((tm, d), row)],
        out_shape=[jax.ShapeDtypeStruct((m, d), f32), jax.ShapeDtypeStruct((m, d), bf16)],
        scratch_shapes=[pltpu.SMEM((2, 2 * tm), i32), pltpu.VMEM((2, 2, tm * ROW_TILE, LANES), u32),
                        pltpu.SemaphoreType.DMA((2,)), pltpu.SemaphoreType.DMA((2,))],
        compiler_params=_cparams(("arbitrary",)),
    )(dest_tiles, x1, tw_t, ln_g, ln_b, y)


def _moe_plan(top_e, n_tok):
    rank, counts = _expert_ranks(top_e.reshape(1, 2 * n_tok))
    counts = counts.reshape(N_EXPERTS).astype(i32)
    padded = (counts + MOE_BLK - 1) // MOE_BLK * MOE_BLK
    pend = jnp.cumsum(padded)
    pstart = pend - padded
    e_flat = top_e.reshape(2 * n_tok)
    start_of = jnp.sum(jnp.where(e_flat[:, None] == jnp.arange(N_EXPERTS, dtype=i32)[None, :],
                                 pstart[None, :], 0), axis=1)
    dest = (start_of + rank.reshape(2 * n_tok)).reshape(2, n_tok)
    n_blocks = -(-(2 * n_tok) // MOE_BLK) + N_EXPERTS
    blk_row = jnp.arange(n_blocks, dtype=i32) * MOE_BLK
    block_e = jnp.minimum(jnp.sum((blk_row[:, None] >= pend[None, :]).astype(i32), axis=1), N_EXPERTS - 1)
    n_active = (pend[-1] // MOE_BLK).reshape(1).astype(i32)
    tm = _pick(n_tok, (TOK_TILE, 128))
    dest_tiles = dest.reshape(2, n_tok // tm, tm).transpose(1, 0, 2).reshape(n_tok // tm, 1, 2 * tm)
    return dest_tiles, block_e.astype(i32), n_active, n_blocks * MOE_BLK


def kernel(x, meta_tokens, w_in, w_gate_lr, b_gate_lr, sb_out_norm, gla_out_norm, w_out, ln1_g, ln1_b,
           w_router, b_router, w_ff_gate, w_ff_up, w_ff_down, ln2_g, ln2_b):
    bsz, seq, d = x.shape
    depth = w_in.shape[0]
    alpha = float((2 * depth) ** 0.25)
    length = N_META + seq
    lp = -(-length // SEQ_ALIGN) * SEQ_ALIGN
    pad = lp - length
    assert (pad - (-length) % GLA_CHUNK) % GLA_CHUNK == 0
    n_tok = bsz * lp

    meta = jnp.broadcast_to(meta_tokens[None].astype(x.dtype), (bsz, N_META, d))
    h = jnp.concatenate([jnp.zeros((bsz, pad, d), x.dtype), meta, x], axis=1).reshape(n_tok, d)
    h16 = h.astype(bf16)

    sb_w = SB_HEADS * SB_HEAD_DIM
    gla_kw = GLA_HEADS * GLA_DK
    gla_vw = GLA_HEADS * GLA_DV
    o_lr = 3 * sb_w + 2 * gla_kw + gla_vw
    o_go = o_lr + GLA_GATE_RANK
    w_main = jnp.concatenate([w_in[:, :, :o_lr], w_in[:, :, o_go:]], axis=2).astype(bf16)
    w_lr = jnp.pad(w_in[:, :, o_lr:o_go], ((0, 0), (0, 0), (0, LANES - GLA_GATE_RANK))).astype(bf16)
    w_g = jnp.pad(w_gate_lr, ((0, 0), (0, LANES - GLA_GATE_RANK), (0, 0))).astype(bf16)
    w_out16 = w_out.astype(bf16)
    wr_hi = w_router.astype(bf16)
    wr_lo = (w_router - wr_hi.astype(f32)).astype(bf16)
    wr = jnp.concatenate([jnp.pad(jnp.concatenate([wr_hi, wr_lo], axis=1), ((0, 0), (0, LANES - 2 * N_EXPERTS))),
                          jnp.pad(wr_hi, ((0, 0), (0, LANES - N_EXPERTS)))], axis=0)
    br = b_router.astype(f32).reshape(N_EXPERTS, 1)
    wg16, wu16, wd16 = w_ff_gate.astype(bf16), w_ff_up.astype(bf16), w_ff_down.astype(bf16)

    for l in range(depth):
        proj = _matmul(h16, w_main[l], bf16)
        log_a = _gla_gate(h16, w_lr[l], w_g[l], b_gate_lr[l].reshape(1, gla_kw))
        proj3 = proj.reshape(bsz, lp, proj.shape[1])
        y_sb = _sb_attention(proj3, sb_out_norm[l].reshape(1, sb_w), pad)
        y_gla = _gla(proj3, log_a.reshape(bsz, lp, gla_kw), gla_out_norm[l].reshape(1, gla_vw), pad)
        x1, x1p, top_e, top_w = _oproj_ln_route(
            y_sb.reshape(n_tok, sb_w), y_gla.reshape(n_tok, gla_vw), w_out16[l, :sb_w], w_out16[l, sb_w:],
            h, ln1_g[l].reshape(1, d), ln1_b[l].reshape(1, d), wr, br, alpha)
        dest_tiles, block_e, n_active, n_rows = _moe_plan(top_e, n_tok)
        xs = _dispatch(x1p, dest_tiles, n_rows)
        y = _expert_ffn(xs, block_e, n_active, wg16[l], wu16[l], wd16[l])
        h, h16 = _combine(dest_tiles, x1, top_w.T, ln2_g[l].reshape(1, d), ln2_b[l].reshape(1, d), y, alpha)

    return h.reshape(bsz, lp, d)[:, pad + N_META:, :]
```

```python
import functools
import math

import jax
import jax.numpy as jnp
from jax import lax
from jax.experimental import pallas as pl
from jax.experimental.pallas import tpu as pltpu

f32 = jnp.float32
bf16 = jnp.bfloat16
u32 = jnp.uint32
i32 = jnp.int32

N_META = 16
SB_HEADS = 8
SB_HEAD_DIM = 128
GLA_HEADS = 4
GLA_DK = 128
GLA_DV = 256
GLA_GATE_RANK = 16
GLA_GATE_TAU = 16.0
GLA_CHUNK = 64
N_EXPERTS = 16
N_GROUPS = 4
EXPERTS_PER_GROUP = 4
LN_EPS = 1e-5
RMS_EPS = 1e-6

LANES = 128
VMEM_LIMIT = 56 * 1024 * 1024

SEQ_ALIGN = 256
SB_TQ = 128
SB_WINDOW = 3
SB_HEADS_PER_STEP = 4
GLA_ROWS = 256
OPROJ_ROWS = 128
MOE_BLK = 512
TOK_TILE = 256
RANK_CHUNK = 512
SB_DEAD_LOGW = 104.0


def _pick(n, cands):
    for c in cands:
        if n % c == 0:
            return c
    raise ValueError(f"no tile in {cands} divides {n}")


def _cparams(sem, vmem=VMEM_LIMIT):
    return pltpu.CompilerParams(dimension_semantics=sem, vmem_limit_bytes=vmem)


def _dot(a, b):
    return jnp.dot(a, b, preferred_element_type=f32)


def _dot_nt(a, b):
    return lax.dot_general(a, b, (((1,), (1,)), ((), ())), preferred_element_type=f32)


def _dot_tn(a, b):
    return lax.dot_general(a, b, (((0,), (0,)), ((), ())), preferred_element_type=f32)


def _pack_pair(lo, hi):
    lo_b = lax.bitcast_convert_type(lo.astype(bf16).astype(f32), u32)
    hi_b = lax.bitcast_convert_type(hi.astype(bf16).astype(f32), u32)
    return (hi_b & jnp.uint32(0xFFFF0000)) | (lo_b >> 16)


def _unpack_pair(w):
    lo = lax.bitcast_convert_type(w << 16, f32)
    hi = lax.bitcast_convert_type(w & jnp.uint32(0xFFFF0000), f32)
    return lo, hi


ROW_TILE = 8


def _store_token_rows(ref, words, t0=0):
    n = words.shape[0]
    for s in range(ROW_TILE):
        ref[pl.ds(t0 * ROW_TILE + s, n, stride=ROW_TILE), :] = words[:, s * LANES:(s + 1) * LANES]


def _load_token_rows(ref, n):
    return [ref[pl.ds(s, n, stride=ROW_TILE), :] for s in range(ROW_TILE)]


def _mm_kernel(x_ref, w_ref, o_ref):
    o_ref[...] = _dot(x_ref[...], w_ref[...]).astype(o_ref.dtype)


def _matmul(x, w, out_dtype):
    m, k = x.shape
    n = w.shape[1]
    tm = _pick(m, (2048, 1024, 512, 256, 128))
    tn = _pick(n, (512, 256, 128))
    return pl.pallas_call(
        _mm_kernel,
        grid=(m // tm, n // tn),
        in_specs=[pl.BlockSpec((tm, k), lambda i, j: (i, 0)),
                  pl.BlockSpec((k, tn), lambda i, j: (0, j))],
        out_specs=pl.BlockSpec((tm, tn), lambda i, j: (i, j)),
        out_shape=jax.ShapeDtypeStruct((m, n), out_dtype),
        compiler_params=_cparams(("parallel", "parallel")),
    )(x, w)


def _gate_kernel(x_ref, wlr_ref, wg_ref, b_ref, o_ref):
    g_lr = _dot(x_ref[...], wlr_ref[...]).astype(bf16)
    y = _dot(g_lr, wg_ref[...]) + b_ref[...]
    log_sig = jnp.minimum(y, 0.0) - jnp.log(1.0 + jnp.exp(-jnp.abs(y)))
    o_ref[...] = log_sig * (1.0 / GLA_GATE_TAU)


def _gla_gate(x, w_lr, w_g, b_g):
    m, k = x.shape
    n = w_g.shape[1]
    tm = _pick(m, (512, 256, 128))
    return pl.pallas_call(
        _gate_kernel,
        grid=(m // tm,),
        in_specs=[pl.BlockSpec((tm, k), lambda i: (i, 0)),
                  pl.BlockSpec((k, LANES), lambda i: (0, 0)),
                  pl.BlockSpec((LANES, n), lambda i: (0, 0)),
                  pl.BlockSpec((1, n), lambda i: (0, 0))],
        out_specs=pl.BlockSpec((tm, n), lambda i: (i, 0)),
        out_shape=jax.ShapeDtypeStruct((m, n), f32),
        compiler_params=_cparams(("parallel",)),
    )(x, w_lr, w_g, b_g)


def _sb_blocks(q_ref, k_ref, v_ref, tri, heads, k0, n_blocks, valid, carries):
    tq, d = SB_TQ, SB_HEAD_DIM
    scale = d ** -0.5
    span = pl.ds(k0, n_blocks * tq)
    cols = [slice(h * d, (h + 1) * d) for h in range(heads)]
    z = [_dot_nt(q_ref[0, :, cs], k_ref[0, span, cs]) * scale for cs in cols]
    log_beta, sums = [], []
    for h in range(heads):
        for j in range(n_blocks):
            zj = z[h][:, j * tq:(j + 1) * tq]
            sp = jnp.maximum(zj, 0.0) + jnp.log(1.0 + jnp.exp(-jnp.abs(zj)))
            log_beta.append(zj - sp)
            sums.append(_dot(jnp.where(valid[j], sp, 0.0).astype(bf16), tri))
    outs, new_carries = [], []
    for h in range(heads):
        carry = carries[h]
        w = [None] * n_blocks
        for j in reversed(range(n_blocks)):
            sm = sums[h * n_blocks + j]
            between = sm[:, :tq] + carry
            w[j] = jnp.where(valid[j], jnp.exp(log_beta[h * n_blocks + j] - between), 0.0).astype(bf16)
            carry = carry + sm[:, tq:]
        outs.append(_dot(w[0] if n_blocks == 1 else jnp.concatenate(w, axis=1), v_ref[0, span, cols[h]]))
        new_carries.append(carry)
    return outs, new_carries


def _sb_kernel(q_ref, k_ref, v_ref, tri_ref, g_ref, o_ref, *, pad, heads):
    tq, d = SB_TQ, SB_HEAD_DIM
    i = pl.program_id(2)
    t0 = i * tq
    wb = jnp.maximum(i - (SB_WINDOW - 1), 0)
    ws = pl.multiple_of(wb * tq, tq)
    key_minus_query = lax.broadcasted_iota(i32, (tq, tq), 1) - lax.broadcasted_iota(i32, (tq, tq), 0)
    col = lax.broadcasted_iota(i32, (tq, tq), 1)
    tri = tri_ref[...]

    def valid_for(k0):
        return jnp.logical_and(key_minus_query < t0 - k0, col >= pad - k0)

    def lowest(carries):
        m = jnp.min(carries[0])
        for c in carries[1:]:
            m = jnp.minimum(m, jnp.min(c))
        return m

    zero = [jnp.zeros((tq, tq), f32)] * heads
    accs, carries = _sb_blocks(q_ref, k_ref, v_ref, tri, heads, ws, SB_WINDOW,
                               [valid_for(ws + j * tq) for j in range(SB_WINDOW)], zero)

    def cond(c):
        jb, cmin, _, _ = c
        return jnp.logical_and(jb >= 0, cmin < SB_DEAD_LOGW)

    def body(c):
        jb, _, accs, carries = c
        k0 = pl.multiple_of(jb * tq, tq)
        outs, carries = _sb_blocks(q_ref, k_ref, v_ref, tri, heads, k0, 1, [valid_for(k0)], list(carries))
        return jb - 1, lowest(carries), tuple(a + o for a, o in zip(accs, outs)), tuple(carries)

    _, _, accs, _ = lax.while_loop(cond, body, (wb - 1, lowest(carries), tuple(accs), tuple(carries)))
    for h in range(heads):
        cs = slice(h * d, (h + 1) * d)
        ms = jnp.mean(accs[h] * accs[h], axis=1, keepdims=True)
        o_ref[0, :, cs] = (accs[h] * lax.rsqrt(ms + RMS_EPS) * g_ref[:, cs]).astype(o_ref.dtype)


def _sb_attention(proj, norm_g, pad):
    b, lp, _ = proj.shape
    tq, heads = SB_TQ, SB_HEADS_PER_STEP
    wid = heads * SB_HEAD_DIM
    groups = SB_HEADS // heads
    later = jnp.arange(tq)[:, None] > jnp.arange(tq)[None, :]
    tri = jnp.concatenate([later, jnp.ones((tq, tq), bool)], axis=1).astype(bf16)
    return pl.pallas_call(
        functools.partial(_sb_kernel, pad=pad, heads=heads),
        grid=(b, groups, lp // tq),
        in_specs=[pl.BlockSpec((1, tq, wid), lambda bi, h, i: (bi, i, h)),
                  pl.BlockSpec((1, lp, wid), lambda bi, h, i: (bi, 0, groups + h)),
                  pl.BlockSpec((1, lp, wid), lambda bi, h, i: (bi, 0, 2 * groups + h)),
                  pl.BlockSpec((tq, 2 * tq), lambda bi, h, i: (0, 0)),
                  pl.BlockSpec((1, wid), lambda bi, h, i: (0, h))],
        out_specs=pl.BlockSpec((1, tq, wid), lambda bi, h, i: (bi, i, h)),
        out_shape=jax.ShapeDtypeStruct((b, lp, SB_HEADS * SB_HEAD_DIM), bf16),
        compiler_params=_cparams(("parallel", "parallel", "parallel")),
    )(proj, proj, proj, tri, norm_g)


def _gla_kernel(q_ref, k_ref, v_ref, la_ref, go_ref, gn_ref, o_ref, st_ref, *, pad):
    c, rows, dk, dv = GLA_CHUNK, GLA_ROWS, GLA_DK, GLA_DV
    s = pl.program_id(1)

    @pl.when(s == 0)
    def _():
        st_ref[...] = jnp.zeros_like(st_ref)

    real = (s * rows + lax.broadcasted_iota(i32, (rows, 1), 0)) >= pad
    r_i = lax.broadcasted_iota(i32, (rows, rows), 0)
    c_i = lax.broadcasted_iota(i32, (rows, rows), 1)
    shift = c.bit_length() - 1
    causal = jnp.logical_and(c_i <= r_i, (r_i >> shift) == (c_i >> shift))
    causal_b = causal.astype(bf16)
    scale = dk ** -0.5

    for h in range(GLA_HEADS):
        ks = slice(h * dk, (h + 1) * dk)
        vs = slice(h * dv, (h + 1) * dv)
        g = jnp.where(real, la_ref[0, :, ks], 0.0)
        g_hi = g.astype(bf16)
        g_lo = (g - g_hi.astype(f32)).astype(bf16)
        bcum = _dot(causal_b, g_hi) + _dot(causal_b, g_lo)
        qf = q_ref[0, :, ks].astype(f32) * scale
        kf = jnp.where(real, k_ref[0, :, ks].astype(f32), 0.0)
        vb = v_ref[0, :, vs]
        q_dec = (qf * jnp.exp(bcum)).astype(bf16)
        k_inv = (kf * jnp.exp(-bcum)).astype(bf16)
        scores = jnp.where(causal, _dot_nt(q_dec, k_inv), 0.0)
        o_intra = _dot(scores.astype(bf16), vb)
        st = st_ref[h]
        o_inter = []
        for n in range(rows // c):
            rs = slice(n * c, (n + 1) * c)
            b_last = bcum[(n + 1) * c - 1:(n + 1) * c, :]
            k_dec = (kf[rs] * jnp.exp(b_last - bcum[rs])).astype(bf16)
            o_inter.append(_dot_nt(q_dec[rs], st.astype(bf16)))
            st = st * jnp.exp(b_last) + _dot_tn(vb[rs], k_dec)
        st_ref[h] = st
        o = o_intra + jnp.concatenate(o_inter, axis=0)
        ms = jnp.mean(o * o, axis=1, keepdims=True)
        gate = go_ref[0, :, vs].astype(f32)
        y = o * lax.rsqrt(ms + RMS_EPS) * gn_ref[:, vs] * (gate * jax.nn.sigmoid(gate))
        o_ref[0, :, vs] = y.astype(o_ref.dtype)


def _gla(proj, log_a, norm_g, pad):
    b, lp, _ = proj.shape
    rows = GLA_ROWS
    kw, vw = GLA_HEADS * GLA_DK, GLA_HEADS * GLA_DV
    return pl.pallas_call(
        functools.partial(_gla_kernel, pad=pad),
        grid=(b, lp // rows),
        in_specs=[pl.BlockSpec((1, rows, kw), lambda bi, s: (bi, s, 6)),
                  pl.BlockSpec((1, rows, kw), lambda bi, s: (bi, s, 7)),
                  pl.BlockSpec((1, rows, vw), lambda bi, s: (bi, s, 4)),
                  pl.BlockSpec((1, rows, kw), lambda bi, s: (bi, s, 0)),
                  pl.BlockSpec((1, rows, vw), lambda bi, s: (bi, s, 5)),
                  pl.BlockSpec((1, vw), lambda bi, s: (0, 0))],
        out_specs=pl.BlockSpec((1, rows, vw), lambda bi, s: (bi, s, 0)),
        out_shape=jax.ShapeDtypeStruct((b, lp, vw), bf16),
        scratch_shapes=[pltpu.VMEM((GLA_HEADS, GLA_DV, GLA_DK), f32)],
        compiler_params=_cparams(("parallel", "arbitrary")),
    )(proj, proj, proj, log_a, proj, norm_g)


def _layer_norm_rows(r, g, b):
    mu = jnp.mean(r, axis=-1, keepdims=True)
    d = r - mu
    var = jnp.mean(d * d, axis=-1, keepdims=True)
    return d * lax.rsqrt(var + LN_EPS) * g + b


def _route(logits):
    m = jnp.max(logits, axis=0, keepdims=True)
    e = jnp.exp(logits - m)
    aff = e / jnp.sum(e, axis=0, keepdims=True)
    rows = [aff[j:j + 1, :] for j in range(N_EXPERTS)]
    scores = []
    for g in range(N_GROUPS):
        a, b, c, d = rows[4 * g:4 * g + 4]
        hi1, lo1 = jnp.maximum(a, b), jnp.minimum(a, b)
        hi2, lo2 = jnp.maximum(c, d), jnp.minimum(c, d)
        scores.append(jnp.maximum(hi1, hi2) + jnp.maximum(jnp.minimum(hi1, hi2), jnp.maximum(lo1, lo2)))
    best = scores[0]
    g_sel = jnp.zeros(best.shape, i32)
    for g in range(1, N_GROUPS):
        better = scores[g] > best
        g_sel = jnp.where(better, g, g_sel)
        best = jnp.where(better, scores[g], best)
    sel = list(rows[0:4])
    for g in range(1, N_GROUPS):
        here = g_sel == g
        sel = [jnp.where(here, rows[4 * g + j], sel[j]) for j in range(4)]
    v1, i1 = sel[0], jnp.zeros(best.shape, i32)
    for j in range(1, 4):
        gt = sel[j] > v1
        i1 = jnp.where(gt, j, i1)
        v1 = jnp.where(gt, sel[j], v1)
    v2, i2 = jnp.full(best.shape, -1.0, f32), jnp.zeros(best.shape, i32)
    for j in range(4):
        cand = jnp.where(i1 == j, -1.0, sel[j])
        gt = cand > v2
        i2 = jnp.where(gt, j, i2)
        v2 = jnp.where(gt, cand, v2)
    s = v1 + v2
    top_e = jnp.concatenate([4 * g_sel + i1, 4 * g_sel + i2], axis=0)
    top_w = jnp.concatenate([v1 / s, v2 / s], axis=0)
    return top_e, top_w


def _oproj_kernel(ysb_ref, ygla_ref, w1_ref, w2_ref, h_ref, g_ref, b_ref, wr_ref, br_ref,
                  x1_ref, x1p_ref, te_ref, tw_ref, *, alpha):
    tm, d = h_ref.shape
    half = d // 2
    n = tm // OPROJ_ROWS
    groups = [slice(p * OPROJ_ROWS, (p + 1) * OPROJ_ROWS) for p in range(n)]
    mixes = [_dot(ysb_ref[rs, :], w1_ref[...]) + _dot(ygla_ref[rs, :], w2_ref[...]) for rs in groups]
    wr = wr_ref[...]
    for p, rs in enumerate(groups):
        x1 = _layer_norm_rows(alpha * h_ref[rs, :] + mixes[p], g_ref[...], b_ref[...])
        x1_ref[rs, :] = x1
        _store_token_rows(x1p_ref, _pack_pair(x1[:, :half], x1[:, half:]), p * OPROJ_ROWS)
        xh = x1.astype(bf16)
        xl = (x1 - xh.astype(f32)).astype(bf16)
        l_t = (_dot(xh, wr[:d]) + _dot(xl, wr[d:])).T
        logits = l_t[:N_EXPERTS] + l_t[N_EXPERTS:2 * N_EXPERTS] + br_ref[...]
        top_e, top_w = _route(logits)
        te_ref[:, rs] = top_e
        tw_ref[:, rs] = top_w


def _oproj_ln_route(y_sb, y_gla, w1, w2, h, ln_g, ln_b, wr, br, alpha):
    m, d = h.shape
    kh = y_sb.shape[1]
    tm = _pick(m, (512, 256, 128))
    row = lambda i: (i, 0)
    fix = lambda i: (0, 0)
    col = lambda i: (0, i)
    return pl.pallas_call(
        functools.partial(_oproj_kernel, alpha=alpha),
        grid=(m // tm,),
        in_specs=[pl.BlockSpec((tm, kh), row), pl.BlockSpec((tm, kh), row),
                  pl.BlockSpec((kh, d), fix), pl.BlockSpec((kh, d), fix),
                  pl.BlockSpec((tm, d), row),
                  pl.BlockSpec((1, d), fix), pl.BlockSpec((1, d), fix),
                  pl.BlockSpec((2 * d, LANES), fix), pl.BlockSpec((N_EXPERTS, 1), fix)],
        out_specs=[pl.BlockSpec((tm, d), row), pl.BlockSpec((tm * ROW_TILE, LANES), row),
                   pl.BlockSpec((2, tm), col), pl.BlockSpec((2, tm), col)],
        out_shape=[jax.ShapeDtypeStruct((m, d), f32), jax.ShapeDtypeStruct((m * ROW_TILE, LANES), u32),
                   jax.ShapeDtypeStruct((2, m), i32), jax.ShapeDtypeStruct((2, m), f32)],
        compiler_params=_cparams(("parallel",)),
    )(y_sb, y_gla, w1, w2, h, ln_g, ln_b, wr, br)


def _rank_kernel(e_ref, rank_ref, cnt_ref, carry_ref, *, chunk):
    @pl.when(pl.program_id(0) == 0)
    def _():
        carry_ref[...] = jnp.zeros_like(carry_ref)

    e = e_ref[...]
    onehot = lax.broadcasted_iota(i32, (N_EXPERTS, chunk), 0) == e
    before = (lax.broadcasted_iota(i32, (chunk, chunk), 0)
              < lax.broadcasted_iota(i32, (chunk, chunk), 1)).astype(bf16)
    seen = _dot(onehot.astype(bf16), before) + carry_ref[...]
    rank_ref[...] = jnp.sum(jnp.where(onehot, seen, 0.0), axis=0, keepdims=True).astype(i32)
    total = carry_ref[...] + jnp.sum(onehot.astype(f32), axis=1, keepdims=True)
    carry_ref[...] = total
    cnt_ref[...] = total


def _expert_ranks(e_flat):
    n = e_flat.shape[1]
    chunk = _pick(n, (RANK_CHUNK, 256, 128))
    return pl.pallas_call(
        functools.partial(_rank_kernel, chunk=chunk),
        grid=(n // chunk,),
        in_specs=[pl.BlockSpec((1, chunk), lambda i: (0, i))],
        out_specs=[pl.BlockSpec((1, chunk), lambda i: (0, i)),
                   pl.BlockSpec((N_EXPERTS, 1), lambda i: (0, 0))],
        out_shape=[jax.ShapeDtypeStruct((1, n), i32), jax.ShapeDtypeStruct((N_EXPERTS, 1), f32)],
        scratch_shapes=[pltpu.VMEM((N_EXPERTS, 1), f32)],
        compiler_params=_cparams(("arbitrary",)),
    )(e_flat)


def _row_copy(src, dst, sem):
    return pltpu.make_async_copy(src, dst, sem)


def _token_rows(ref, t, n=1):
    start = t * ROW_TILE
    if not isinstance(start, int):
        start = pl.multiple_of(start, ROW_TILE)
    return ref.at[pl.ds(start, n * ROW_TILE)]


def _dispatch_kernel(dest_hbm, x_ref, zeros_hbm, xs_hbm, dsm, sem_i, sem_d, *, tm, n_tiles):
    del zeros_hbm
    i = pl.program_id(0)
    slot = i % 2

    def idx_copy(t, sl):
        return _row_copy(dest_hbm.at[t], dsm.at[pl.ds(sl, 1)], sem_i.at[sl])

    @pl.when(i == 0)
    def _():
        idx_copy(0, 0).start()

    idx_copy(i, slot).wait()

    @pl.when(i + 1 < n_tiles)
    def _():
        idx_copy(i + 1, 1 - slot).start()

    def body(r, carry):
        for k in range(2):
            d = dsm[slot, k * tm + r]
            _row_copy(_token_rows(x_ref, r), _token_rows(xs_hbm, d), sem_d).start(priority=k)
        return carry

    lax.fori_loop(0, tm, body, 0, unroll=8)
    for _ in range(2):
        _row_copy(x_ref, _token_rows(xs_hbm, 0, tm), sem_d).wait()


def _dispatch(x_rows, dest_tiles, n_slots):
    n_tiles = dest_tiles.shape[0]
    tm = dest_tiles.shape[2] // 2
    zeros = jnp.zeros((n_slots * ROW_TILE, LANES), u32)
    return pl.pallas_call(
        functools.partial(_dispatch_kernel, tm=tm, n_tiles=n_tiles),
        grid=(n_tiles,),
        in_specs=[pl.BlockSpec(memory_space=pl.ANY),
                  pl.BlockSpec((tm * ROW_TILE, LANES), lambda i: (i, 0)),
                  pl.BlockSpec(memory_space=pl.ANY)],
        out_specs=pl.BlockSpec(memory_space=pl.ANY),
        out_shape=jax.ShapeDtypeStruct(zeros.shape, u32),
        scratch_shapes=[pltpu.SMEM((2, 2 * tm), i32), pltpu.SemaphoreType.DMA((2,)), pltpu.SemaphoreType.DMA(())],
        input_output_aliases={2: 0},
        compiler_params=_cparams(("arbitrary",)),
    )(dest_tiles, x_rows, zeros)


def _ffn_kernel(be_ref, na_ref, xs_ref, wg_ref, wu_ref, wd_ref, y_ref):
    del be_ref
    blk = pl.program_id(0)

    @pl.when(blk < na_ref[0])
    def _():
        parts = [_unpack_pair(w) for w in _load_token_rows(xs_ref, MOE_BLK)]
        lo = jnp.concatenate([p[0].astype(bf16) for p in parts], axis=1)
        hi = jnp.concatenate([p[1].astype(bf16) for p in parts], axis=1)
        half = lo.shape[1]
        gate = _dot(lo, wg_ref[0, :half, :]) + _dot(hi, wg_ref[0, half:, :])
        up = _dot(lo, wu_ref[0, :half, :]) + _dot(hi, wu_ref[0, half:, :])
        hid = (gate * jax.nn.sigmoid(gate) * up).astype(bf16)
        out = _dot(hid, wd_ref[0])
        _store_token_rows(y_ref, _pack_pair(out[:, :half], out[:, half:]))

    @pl.when(blk >= na_ref[0])
    def _():
        y_ref[...] = jnp.zeros_like(y_ref)


def _expert_ffn(xs, block_e, n_active, wg, wu, wd):
    d, dff = wg.shape[1], wg.shape[2]
    n_blocks = xs.shape[0] // (MOE_BLK * ROW_TILE)
    grid_spec = pltpu.PrefetchScalarGridSpec(
        num_scalar_prefetch=2,
        grid=(n_blocks,),
        in_specs=[pl.BlockSpec((MOE_BLK * ROW_TILE, LANES), lambda b, be, na: (b, 0)),
                  pl.BlockSpec((1, d, dff), lambda b, be, na: (be[b], 0, 0)),
                  pl.BlockSpec((1, d, dff), lambda b, be, na: (be[b], 0, 0)),
                  pl.BlockSpec((1, dff, d), lambda b, be, na: (be[b], 0, 0))],
        out_specs=pl.BlockSpec((MOE_BLK * ROW_TILE, LANES), lambda b, be, na: (b, 0)),
    )
    return pl.pallas_call(
        _ffn_kernel,
        grid_spec=grid_spec,
        out_shape=jax.ShapeDtypeStruct(xs.shape, u32),
        compiler_params=_cparams(("arbitrary",)),
    )(block_e, n_active, xs, wg, wu, wd)


def _combine_kernel(dest_hbm, x1_ref, tw_ref, g_ref, b_ref, y_hbm, h32_ref, h16_ref,
                    dsm, ybuf, sem_i, sem_g, *, tm, n_tiles, alpha):
    i = pl.program_id(0)
    slot = i % 2

    def idx_copy(t, sl):
        return _row_copy(dest_hbm.at[t], dsm.at[pl.ds(sl, 1)], sem_i.at[sl])

    def gather(sl):
        def body(r, carry):
            for k in range(2):
                d = dsm[sl, k * tm + r]
                _row_copy(_token_rows(y_hbm, d), _token_rows(ybuf.at[sl, k], r), sem_g.at[sl]).start(priority=k)
            return carry

        lax.fori_loop(0, tm, body, 0, unroll=8)

    @pl.when(i == 0)
    def _():
        idx_copy(0, 0).start()
        idx_copy(0, 0).wait()
        gather(0)
        if n_tiles > 1:
            idx_copy(1, 1).start()

    @pl.when(i + 1 < n_tiles)
    def _():
        idx_copy(i + 1, 1 - slot).wait()
        gather(1 - slot)

    @pl.when(i + 2 < n_tiles)
    def _():
        idx_copy(i + 2, slot).start()

    for k in range(2):
        _row_copy(_token_rows(y_hbm, 0, tm), ybuf.at[slot, k], sem_g.at[slot]).wait()

    tw = tw_ref[...]
    w0, w1 = tw[:, 0:1], tw[:, 1:2]
    rows0 = _load_token_rows(ybuf.at[slot, 0], tm)
    rows1 = _load_token_rows(ybuf.at[slot, 1], tm)
    half = ROW_TILE * LANES
    res = [None] * (2 * ROW_TILE)
    for s in range(ROW_TILE):
        lo0, hi0 = _unpack_pair(rows0[s])
        lo1, hi1 = _unpack_pair(rows1[s])
        res[s] = alpha * x1_ref[:, s * LANES:(s + 1) * LANES] + (w0 * lo0 + w1 * lo1)
        res[ROW_TILE + s] = alpha * x1_ref[:, half + s * LANES:half + (s + 1) * LANES] + (w0 * hi0 + w1 * hi1)
    inv_d = 1.0 / (2 * half)
    mu = jnp.sum(functools.reduce(jnp.add, res), axis=-1, keepdims=True) * inv_d
    res = [r - mu for r in res]
    var = jnp.sum(functools.reduce(jnp.add, [r * r for r in res]), axis=-1, keepdims=True) * inv_d
    rstd = lax.rsqrt(var + LN_EPS)
    for c, r in enumerate(res):
        cs = slice(c * LANES, (c + 1) * LANES)
        o = r * rstd * g_ref[:, cs] + b_ref[:, cs]
        h32_ref[:, cs] = o
        h16_ref[:, cs] = o.astype(bf16)


def _combine(dest_tiles, x1, tw_t, ln_g, ln_b, y, alpha):
    m, d = x1.shape
    n_tiles = dest_tiles.shape[0]
    tm = dest_tiles.shape[2] // 2
    row = lambda i: (i, 0)
    fix = lambda i: (0, 0)
    return pl.pallas_call(
        functools.partial(_combine_kernel, tm=tm, n_tiles=n_tiles, alpha=alpha),
        grid=(n_tiles,),
        in_specs=[pl.BlockSpec(memory_space=pl.ANY),
                  pl.BlockSpec((tm, d), row),
                  pl.BlockSpec((tm, 2), row),
                  pl.BlockSpec((1, d), fix), pl.BlockSpec((1, d), fix),
                  pl.BlockSpec(memory_space=pl.ANY)],
        out_specs=[pl.BlockSpec((tm, d), row), pl.BlockSpec((tm, d), row)],
        out_shape=[jax.ShapeDtypeStruct((m, d), f32), jax.ShapeDtypeStruct((m, d), bf16)],
        scratch_shapes=[pltpu.SMEM((2, 2 * tm), i32), pltpu.VMEM((2, 2, tm * ROW_TILE, LANES), u32),
                        pltpu.SemaphoreType.DMA((2,)), pltpu.SemaphoreType.DMA((2,))],
        compiler_params=_cparams(("arbitrary",)),
    )(dest_tiles, x1, tw_t, ln_g, ln_b, y)


def _moe_plan(top_e, n_tok):
    rank, counts = _expert_ranks(top_e.reshape(1, 2 * n_tok))
    counts = counts.reshape(N_EXPERTS).astype(i32)
    padded = (counts + MOE_BLK - 1) // MOE_BLK * MOE_BLK
    pend = jnp.cumsum(padded)
    pstart = pend - padded
    e_flat = top_e.reshape(2 * n_tok)
    start_of = jnp.sum(jnp.where(e_flat[:, None] == jnp.arange(N_EXPERTS, dtype=i32)[None, :],
                                 pstart[None, :], 0), axis=1)
    dest = (start_of + rank.reshape(2 * n_tok)).reshape(2, n_tok)
    n_blocks = -(-(2 * n_tok) // MOE_BLK) + N_EXPERTS
    blk_row = jnp.arange(n_blocks, dtype=i32) * MOE_BLK
    block_e = jnp.minimum(jnp.sum((blk_row[:, None] >= pend[None, :]).astype(i32), axis=1), N_EXPERTS - 1)
    n_active = (pend[-1] // MOE_BLK).reshape(1).astype(i32)
    tm = _pick(n_tok, (TOK_TILE, 128))
    dest_tiles = dest.reshape(2, n_tok // tm, tm).transpose(1, 0, 2).reshape(n_tok // tm, 1, 2 * tm)
    return dest_tiles, block_e.astype(i32), n_active, n_blocks * MOE_BLK


def kernel(x, meta_tokens, w_in, w_gate_lr, b_gate_lr, sb_out_norm, gla_out_norm, w_out, ln1_g, ln1_b,
           w_router, b_router, w_ff_gate, w_ff_up, w_ff_down, ln2_g, ln2_b):
    bsz, seq, d = x.shape
    depth = w_in.shape[0]
    alpha = float((2 * depth) ** 0.25)
    length = N_META + seq
    lp = -(-length // SEQ_ALIGN) * SEQ_ALIGN
    pad = lp - length
    assert (pad - (-length) % GLA_CHUNK) % GLA_CHUNK == 0
    n_tok = bsz * lp

    meta = jnp.broadcast_to(meta_tokens[None].astype(x.dtype), (bsz, N_META, d))
    h = jnp.concatenate([jnp.zeros((bsz, pad, d), x.dtype), meta, x], axis=1).reshape(n_tok, d)
    h16 = h.astype(bf16)

    sb_w = SB_HEADS * SB_HEAD_DIM
    gla_kw = GLA_HEADS * GLA_DK
    gla_vw = GLA_HEADS * GLA_DV
    o_lr = 3 * sb_w + 2 * gla_kw + gla_vw
    o_go = o_lr + GLA_GATE_RANK
    w_main = jnp.concatenate([w_in[:, :, :o_lr], w_in[:, :, o_go:]], axis=2).astype(bf16)
    w_lr = jnp.pad(w_in[:, :, o_lr:o_go], ((0, 0), (0, 0), (0, LANES - GLA_GATE_RANK))).astype(bf16)
    w_g = jnp.pad(w_gate_lr, ((0, 0), (0, LANES - GLA_GATE_RANK), (0, 0))).astype(bf16)
    w_out16 = w_out.astype(bf16)
    wr_hi = w_router.astype(bf16)
    wr_lo = (w_router - wr_hi.astype(f32)).astype(bf16)
    wr = jnp.concatenate([jnp.pad(jnp.concatenate([wr_hi, wr_lo], axis=1), ((0, 0), (0, LANES - 2 * N_EXPERTS))),
                          jnp.pad(wr_hi, ((0, 0), (0, LANES - N_EXPERTS)))], axis=0)
    br = b_router.astype(f32).reshape(N_EXPERTS, 1)
    wg16, wu16, wd16 = w_ff_gate.astype(bf16), w_ff_up.astype(bf16), w_ff_down.astype(bf16)

    for l in range(depth):
        proj = _matmul(h16, w_main[l], bf16)
        log_a = _gla_gate(h16, w_lr[l], w_g[l], b_gate_lr[l].reshape(1, gla_kw))
        proj3 = proj.reshape(bsz, lp, proj.shape[1])
        y_sb = _sb_attention(proj3, sb_out_norm[l].reshape(1, sb_w), pad)
        y_gla = _gla(proj3, log_a.reshape(bsz, lp, gla_kw), gla_out_norm[l].reshape(1, gla_vw), pad)
        x1, x1p, top_e, top_w = _oproj_ln_route(
            y_sb.reshape(n_tok, sb_w), y_gla.reshape(n_tok, gla_vw), w_out16[l, :sb_w], w_out16[l, sb_w:],
            h, ln1_g[l].reshape(1, d), ln1_b[l].reshape(1, d), wr, br, alpha)
        dest_tiles, block_e, n_active, n_rows = _moe_plan(top_e, n_tok)
        xs = _dispatch(x1p, dest_tiles, n_rows)
        y = _expert_ffn(xs, block_e, n_active, wg16[l], wu16[l], wd16[l])
        h, h16 = _combine(dest_tiles, x1, top_w.T, ln2_g[l].reshape(1, d), ln2_b[l].reshape(1, d), y, alpha)

    return h.reshape(bsz, lp, d)[:, pad + N_META:, :]
```

```python
import functools
import math

import jax
import jax.numpy as jnp
from jax import lax
from jax.experimental import pallas as pl
from jax.experimental.pallas import tpu as pltpu

f32 = jnp.float32
bf16 = jnp.bfloat16
u32 = jnp.uint32
i32 = jnp.int32

N_META = 16
SB_HEADS = 8
SB_HEAD_DIM = 128
GLA_HEADS = 4
GLA_DK = 128
GLA_DV = 256
GLA_GATE_RANK = 16
GLA_GATE_TAU = 16.0
GLA_CHUNK = 64
N_EXPERTS = 16
N_GROUPS = 4
EXPERTS_PER_GROUP = 4
LN_EPS = 1e-5
RMS_EPS = 1e-6

LANES = 128
VMEM_LIMIT = 56 * 1024 * 1024

SEQ_ALIGN = 256
SB_TQ = 128
SB_WINDOW = 3
SB_HEADS_PER_STEP = 4
GLA_ROWS = 256
OPROJ_ROWS = 128
MOE_BLK = 512
TOK_TILE = 256
RANK_CHUNK = 512
SB_DEAD_LOGW = 104.0


def _pick(n, cands):
    for c in cands:
        if n % c == 0:
            return c
    raise ValueError(f"no tile in {cands} divides {n}")


def _cparams(sem, vmem=VMEM_LIMIT):
    return pltpu.CompilerParams(dimension_semantics=sem, vmem_limit_bytes=vmem)


def _dot(a, b):
    return jnp.dot(a, b, preferred_element_type=f32)


def _dot_nt(a, b):
    return lax.dot_general(a, b, (((1,), (1,)), ((), ())), preferred_element_type=f32)


def _dot_tn(a, b):
    return lax.dot_general(a, b, (((0,), (0,)), ((), ())), preferred_element_type=f32)


def _pack_pair(lo, hi):
    lo_b = lax.bitcast_convert_type(lo.astype(bf16).astype(f32), u32)
    hi_b = lax.bitcast_convert_type(hi.astype(bf16).astype(f32), u32)
    return (hi_b & jnp.uint32(0xFFFF0000)) | (lo_b >> 16)


def _unpack_pair(w):
    lo = lax.bitcast_convert_type(w << 16, f32)
    hi = lax.bitcast_convert_type(w & jnp.uint32(0xFFFF0000), f32)
    return lo, hi


ROW_TILE = 8


def _store_token_rows(ref, words, t0=0):
    n = words.shape[0]
    for s in range(ROW_TILE):
        ref[pl.ds(t0 * ROW_TILE + s, n, stride=ROW_TILE), :] = words[:, s * LANES:(s + 1) * LANES]


def _load_token_rows(ref, n):
    return [ref[pl.ds(s, n, stride=ROW_TILE), :] for s in range(ROW_TILE)]


def _mm_kernel(x_ref, w_ref, o_ref):
    o_ref[...] = _dot(x_ref[...], w_ref[...]).astype(o_ref.dtype)


def _matmul(x, w, out_dtype):
    m, k = x.shape
    n = w.shape[1]
    tm = _pick(m, (2048, 1024, 512, 256, 128))
    tn = _pick(n, (512, 256, 128))
    return pl.pallas_call(
        _mm_kernel,
        grid=(m // tm, n // tn),
        in_specs=[pl.BlockSpec((tm, k), lambda i, j: (i, 0)),
                  pl.BlockSpec((k, tn), lambda i, j: (0, j))],
        out_specs=pl.BlockSpec((tm, tn), lambda i, j: (i, j)),
        out_shape=jax.ShapeDtypeStruct((m, n), out_dtype),
        compiler_params=_cparams(("parallel", "parallel")),
    )(x, w)


def _gate_kernel(x_ref, wlr_ref, wg_ref, b_ref, o_ref):
    g_lr = _dot(x_ref[...], wlr_ref[...]).astype(bf16)
    y = _dot(g_lr, wg_ref[...]) + b_ref[...]
    log_sig = jnp.minimum(y, 0.0) - jnp.log(1.0 + jnp.exp(-jnp.abs(y)))
    o_ref[...] = log_sig * (1.0 / GLA_GATE_TAU)


def _gla_gate(x, w_lr, w_g, b_g):
    m, k = x.shape
    n = w_g.shape[1]
    tm = _pick(m, (512, 256, 128))
    return pl.pallas_call(
        _gate_kernel,
        grid=(m // tm,),
        in_specs=[pl.BlockSpec((tm, k), lambda i: (i, 0)),
                  pl.BlockSpec((k, LANES), lambda i: (0, 0)),
                  pl.BlockSpec((LANES, n), lambda i: (0, 0)),
                  pl.BlockSpec((1, n), lambda i: (0, 0))],
        out_specs=pl.BlockSpec((tm, n), lambda i: (i, 0)),
        out_shape=jax.ShapeDtypeStruct((m, n), f32),
        compiler_params=_cparams(("parallel",)),
    )(x, w_lr, w_g, b_g)


def _sb_blocks(q_ref, k_ref, v_ref, tri, heads, k0, n_blocks, valid, carries):
    tq, d = SB_TQ, SB_HEAD_DIM
    scale = d ** -0.5
    span = pl.ds(k0, n_blocks * tq)
    cols = [slice(h * d, (h + 1) * d) for h in range(heads)]
    z = [_dot_nt(q_ref[0, :, cs], k_ref[0, span, cs]) * scale for cs in cols]
    log_beta, sums = [], []
    for h in range(heads):
        for j in range(n_blocks):
            zj = z[h][:, j * tq:(j + 1) * tq]
            sp = jnp.maximum(zj, 0.0) + jnp.log(1.0 + jnp.exp(-jnp.abs(zj)))
            log_beta.append(zj - sp)
            sums.append(_dot(jnp.where(valid[j], sp, 0.0).astype(bf16), tri))
    outs, new_carries = [], []
    for h in range(heads):
        carry = carries[h]
        w = [None] * n_blocks
        for j in reversed(range(n_blocks)):
            sm = sums[h * n_blocks + j]
            between = sm[:, :tq] + carry
            w[j] = jnp.where(valid[j], jnp.exp(log_beta[h * n_blocks + j] - between), 0.0).astype(bf16)
            carry = carry + sm[:, tq:]
        outs.append(_dot(w[0] if n_blocks == 1 else jnp.concatenate(w, axis=1), v_ref[0, span, cols[h]]))
        new_carries.append(carry)
    return outs, new_carries


def _sb_kernel(q_ref, k_ref, v_ref, tri_ref, g_ref, o_ref, *, pad, heads):
    tq, d = SB_TQ, SB_HEAD_DIM
    i = pl.program_id(2)
    t0 = i * tq
    wb = jnp.maximum(i - (SB_WINDOW - 1), 0)
    ws = pl.multiple_of(wb * tq, tq)
    key_minus_query = lax.broadcasted_iota(i32, (tq, tq), 1) - lax.broadcasted_iota(i32, (tq, tq), 0)
    col = lax.broadcasted_iota(i32, (tq, tq), 1)
    tri = tri_ref[...]

    def valid_for(k0):
        return jnp.logical_and(key_minus_query < t0 - k0, col >= pad - k0)

    def lowest(carries):
        m = jnp.min(carries[0])
        for c in carries[1:]:
            m = jnp.minimum(m, jnp.min(c))
        return m

    zero = [jnp.zeros((tq, tq), f32)] * heads
    accs, carries = _sb_blocks(q_ref, k_ref, v_ref, tri, heads, ws, SB_WINDOW,
                               [valid_for(ws + j * tq) for j in range(SB_WINDOW)], zero)

    def cond(c):
        jb, cmin, _, _ = c
        return jnp.logical_and(jb >= 0, cmin < SB_DEAD_LOGW)

    def body(c):
        jb, _, accs, carries = c
        k0 = pl.multiple_of(jb * tq, tq)
        outs, carries = _sb_blocks(q_ref, k_ref, v_ref, tri, heads, k0, 1, [valid_for(k0)], list(carries))
        return jb - 1, lowest(carries), tuple(a + o for a, o in zip(accs, outs)), tuple(carries)

    _, _, accs, _ = lax.while_loop(cond, body, (wb - 1, lowest(carries), tuple(accs), tuple(carries)))
    for h in range(heads):
        cs = slice(h * d, (h + 1) * d)
        ms = jnp.mean(accs[h] * accs[h], axis=1, keepdims=True)
        o_ref[0, :, cs] = (accs[h] * lax.rsqrt(ms + RMS_EPS) * g_ref[:, cs]).astype(o_ref.dtype)


def _sb_attention(proj, norm_g, pad):
    b, lp, _ = proj.shape
    tq, heads = SB_TQ, SB_HEADS_PER_STEP
    wid = heads * SB_HEAD_DIM
    groups = SB_HEADS // heads
    later = jnp.arange(tq)[:, None] > jnp.arange(tq)[None, :]
    tri = jnp.concatenate([later, jnp.ones((tq, tq), bool)], axis=1).astype(bf16)
    return pl.pallas_call(
        functools.partial(_sb_kernel, pad=pad, heads=heads),
        grid=(b, groups, lp // tq),
        in_specs=[pl.BlockSpec((1, tq, wid), lambda bi, h, i: (bi, i, h)),
                  pl.BlockSpec((1, lp, wid), lambda bi, h, i: (bi, 0, groups + h)),
                  pl.BlockSpec((1, lp, wid), lambda bi, h, i: (bi, 0, 2 * groups + h)),
                  pl.BlockSpec((tq, 2 * tq), lambda bi, h, i: (0, 0)),
                  pl.BlockSpec((1, wid), lambda bi, h, i: (0, h))],
        out_specs=pl.BlockSpec((1, tq, wid), lambda bi, h, i: (bi, i, h)),
        out_shape=jax.ShapeDtypeStruct((b, lp, SB_HEADS * SB_HEAD_DIM), bf16),
        compiler_params=_cparams(("parallel", "parallel", "parallel")),
    )(proj, proj, proj, tri, norm_g)


def _gla_kernel(q_ref, k_ref, v_ref, la_ref, go_ref, gn_ref, o_ref, st_ref, *, pad):
    c, rows, dk, dv = GLA_CHUNK, GLA_ROWS, GLA_DK, GLA_DV
    s = pl.program_id(1)

    @pl.when(s == 0)
    def _():
        st_ref[...] = jnp.zeros_like(st_ref)

    real = (s * rows + lax.broadcasted_iota(i32, (rows, 1), 0)) >= pad
    r_i = lax.broadcasted_iota(i32, (rows, rows), 0)
    c_i = lax.broadcasted_iota(i32, (rows, rows), 1)
    shift = c.bit_length() - 1
    causal = jnp.logical_and(c_i <= r_i, (r_i >> shift) == (c_i >> shift))
    causal_b = causal.astype(bf16)
    scale = dk ** -0.5

    for h in range(GLA_HEADS):
        ks = slice(h * dk, (h + 1) * dk)
        vs = slice(h * dv, (h + 1) * dv)
        g = jnp.where(real, la_ref[0, :, ks], 0.0)
        g_hi = g.astype(bf16)
        g_lo = (g - g_hi.astype(f32)).astype(bf16)
        bcum = _dot(causal_b, g_hi) + _dot(causal_b, g_lo)
        qf = q_ref[0, :, ks].astype(f32) * scale
        kf = jnp.where(real, k_ref[0, :, ks].astype(f32), 0.0)
        vb = v_ref[0, :, vs]
        q_dec = (qf * jnp.exp(bcum)).astype(bf16)
        k_inv = (kf * jnp.exp(-bcum)).astype(bf16)
        scores = jnp.where(causal, _dot_nt(q_dec, k_inv), 0.0)
        o_intra = _dot(scores.astype(bf16), vb)
        st = st_ref[h]
        o_inter = []
        for n in range(rows // c):
            rs = slice(n * c, (n + 1) * c)
            b_last = bcum[(n + 1) * c - 1:(n + 1) * c, :]
            k_dec = (kf[rs] * jnp.exp(b_last - bcum[rs])).astype(bf16)
            o_inter.append(_dot_nt(q_dec[rs], st.astype(bf16)))
            st = st * jnp.exp(b_last) + _dot_tn(vb[rs], k_dec)
        st_ref[h] = st
        o = o_intra + jnp.concatenate(o_inter, axis=0)
        ms = jnp.mean(o * o, axis=1, keepdims=True)
        gate = go_ref[0, :, vs].astype(f32)
        y = o * lax.rsqrt(ms + RMS_EPS) * gn_ref[:, vs] * (gate * jax.nn.sigmoid(gate))
        o_ref[0, :, vs] = y.astype(o_ref.dtype)


def _gla(proj, log_a, norm_g, pad):
    b, lp, _ = proj.shape
    rows = GLA_ROWS
    kw, vw = GLA_HEADS * GLA_DK, GLA_HEADS * GLA_DV
    return pl.pallas_call(
        functools.partial(_gla_kernel, pad=pad),
        grid=(b, lp // rows),
        in_specs=[pl.BlockSpec((1, rows, kw), lambda bi, s: (bi, s, 6)),
                  pl.BlockSpec((1, rows, kw), lambda bi, s: (bi, s, 7)),
                  pl.BlockSpec((1, rows, vw), lambda bi, s: (bi, s, 4)),
                  pl.BlockSpec((1, rows, kw), lambda bi, s: (bi, s, 0)),
                  pl.BlockSpec((1, rows, vw), lambda bi, s: (bi, s, 5)),
                  pl.BlockSpec((1, vw), lambda bi, s: (0, 0))],
        out_specs=pl.BlockSpec((1, rows, vw), lambda bi, s: (bi, s, 0)),
        out_shape=jax.ShapeDtypeStruct((b, lp, vw), bf16),
        scratch_shapes=[pltpu.VMEM((GLA_HEADS, GLA_DV, GLA_DK), f32)],
        compiler_params=_cparams(("parallel", "arbitrary")),
    )(proj, proj, proj, log_a, proj, norm_g)


def _layer_norm_rows(r, g, b):
    mu = jnp.mean(r, axis=-1, keepdims=True)
    d = r - mu
    var = jnp.mean(d * d, axis=-1, keepdims=True)
    return d * lax.rsqrt(var + LN_EPS) * g + b


def _route(logits):
    m = jnp.max(logits, axis=0, keepdims=True)
    e = jnp.exp(logits - m)
    aff = e / jnp.sum(e, axis=0, keepdims=True)
    rows = [aff[j:j + 1, :] for j in range(N_EXPERTS)]
    scores = []
    for g in range(N_GROUPS):
        a, b, c, d = rows[4 * g:4 * g + 4]
        hi1, lo1 = jnp.maximum(a, b), jnp.minimum(a, b)
        hi2, lo2 = jnp.maximum(c, d), jnp.minimum(c, d)
        scores.append(jnp.maximum(hi1, hi2) + jnp.maximum(jnp.minimum(hi1, hi2), jnp.maximum(lo1, lo2)))
    best = scores[0]
    g_sel = jnp.zeros(best.shape, i32)
    for g in range(1, N_GROUPS):
        better = scores[g] > best
        g_sel = jnp.where(better, g, g_sel)
        best = jnp.where(better, scores[g], best)
    sel = list(rows[0:4])
    for g in range(1, N_GROUPS):
        here = g_sel == g
        sel = [jnp.where(here, rows[4 * g + j], sel[j]) for j in range(4)]
    v1, i1 = sel[0], jnp.zeros(best.shape, i32)
    for j in range(1, 4):
        gt = sel[j] > v1
        i1 = jnp.where(gt, j, i1)
        v1 = jnp.where(gt, sel[j], v1)
    v2, i2 = jnp.full(best.shape, -1.0, f32), jnp.zeros(best.shape, i32)
    for j in range(4):
        cand = jnp.where(i1 == j, -1.0, sel[j])
        gt = cand > v2
        i2 = jnp.where(gt, j, i2)
        v2 = jnp.where(gt, cand, v2)
    s = v1 + v2
    top_e = jnp.concatenate([4 * g_sel + i1, 4 * g_sel + i2], axis=0)
    top_w = jnp.concatenate([v1 / s, v2 / s], axis=0)
    return top_e, top_w


def _oproj_kernel(ysb_ref, ygla_ref, w1_ref, w2_ref, h_ref, g_ref, b_ref, wr_ref, br_ref,
                  x1_ref, x1p_ref, te_ref, tw_ref, *, alpha):
    tm, d = h_ref.shape
    half = d // 2
    n = tm // OPROJ_ROWS
    groups = [slice(p * OPROJ_ROWS, (p + 1) * OPROJ_ROWS) for p in range(n)]
    mixes = [_dot(ysb_ref[rs, :], w1_ref[...]) + _dot(ygla_ref[rs, :], w2_ref[...]) for rs in groups]
    wr = wr_ref[...]
    for p, rs in enumerate(groups):
        x1 = _layer_norm_rows(alpha * h_ref[rs, :] + mixes[p], g_ref[...], b_ref[...])
        x1_ref[rs, :] = x1
        _store_token_rows(x1p_ref, _pack_pair(x1[:, :half], x1[:, half:]), p * OPROJ_ROWS)
        xh = x1.astype(bf16)
        xl = (x1 - xh.astype(f32)).astype(bf16)
        l_t = (_dot(xh, wr[:d]) + _dot(xl, wr[d:])).T
        logits = l_t[:N_EXPERTS] + l_t[N_EXPERTS:2 * N_EXPERTS] + br_ref[...]
        top_e, top_w = _route(logits)
        te_ref[:, rs] = top_e
        tw_ref[:, rs] = top_w


def _oproj_ln_route(y_sb, y_gla, w1, w2, h, ln_g, ln_b, wr, br, alpha):
    m, d = h.shape
    kh = y_sb.shape[1]
    tm = _pick(m, (512, 256, 128))
    row = lambda i: (i, 0)
    fix = lambda i: (0, 0)
    col = lambda i: (0, i)
    return pl.pallas_call(
        functools.partial(_oproj_kernel, alpha=alpha),
        grid=(m // tm,),
        in_specs=[pl.BlockSpec((tm, kh), row), pl.BlockSpec((tm, kh), row),
                  pl.BlockSpec((kh, d), fix), pl.BlockSpec((kh, d), fix),
                  pl.BlockSpec((tm, d), row),
                  pl.BlockSpec((1, d), fix), pl.BlockSpec((1, d), fix),
                  pl.BlockSpec((2 * d, LANES), fix), pl.BlockSpec((N_EXPERTS, 1), fix)],
        out_specs=[pl.BlockSpec((tm, d), row), pl.BlockSpec((tm * ROW_TILE, LANES), row),
                   pl.BlockSpec((2, tm), col), pl.BlockSpec((2, tm), col)],
        out_shape=[jax.ShapeDtypeStruct((m, d), f32), jax.ShapeDtypeStruct((m * ROW_TILE, LANES), u32),
                   jax.ShapeDtypeStruct((2, m), i32), jax.ShapeDtypeStruct((2, m), f32)],
        compiler_params=_cparams(("parallel",)),
    )(y_sb, y_gla, w1, w2, h, ln_g, ln_b, wr, br)


def _rank_kernel(e_ref, rank_ref, cnt_ref, carry_ref, *, chunk):
    @pl.when(pl.program_id(0) == 0)
    def _():
        carry_ref[...] = jnp.zeros_like(carry_ref)

    e = e_ref[...]
    onehot = lax.broadcasted_iota(i32, (N_EXPERTS, chunk), 0) == e
    before = (lax.broadcasted_iota(i32, (chunk, chunk), 0)
              < lax.broadcasted_iota(i32, (chunk, chunk), 1)).astype(bf16)
    seen = _dot(onehot.astype(bf16), before) + carry_ref[...]
    rank_ref[...] = jnp.sum(jnp.where(onehot, seen, 0.0), axis=0, keepdims=True).astype(i32)
    total = carry_ref[...] + jnp.sum(onehot.astype(f32), axis=1, keepdims=True)
    carry_ref[...] = total
    cnt_ref[...] = total


def _expert_ranks(e_flat):
    n = e_flat.shape[1]
    chunk = _pick(n, (RANK_CHUNK, 256, 128))
    return pl.pallas_call(
        functools.partial(_rank_kernel, chunk=chunk),
        grid=(n // chunk,),
        in_specs=[pl.BlockSpec((1, chunk), lambda i: (0, i))],
        out_specs=[pl.BlockSpec((1, chunk), lambda i: (0, i)),
                   pl.BlockSpec((N_EXPERTS, 1), lambda i: (0, 0))],
        out_shape=[jax.ShapeDtypeStruct((1, n), i32), jax.ShapeDtypeStruct((N_EXPERTS, 1), f32)],
        scratch_shapes=[pltpu.VMEM((N_EXPERTS, 1), f32)],
        compiler_params=_cparams(("arbitrary",)),
    )(e_flat)


def _row_copy(src, dst, sem):
    return pltpu.make_async_copy(src, dst, sem)


def _token_rows(ref, t, n=1):
    start = t * ROW_TILE
    if not isinstance(start, int):
        start = pl.multiple_of(start, ROW_TILE)
    return ref.at[pl.ds(start, n * ROW_TILE)]


def _dispatch_kernel(dest_hbm, x_ref, zeros_hbm, xs_hbm, dsm, sem_i, sem_d, *, tm, n_tiles):
    del zeros_hbm
    i = pl.program_id(0)
    slot = i % 2

    def idx_copy(t, sl):
        return _row_copy(dest_hbm.at[t], dsm.at[pl.ds(sl, 1)], sem_i.at[sl])

    @pl.when(i == 0)
    def _():
        idx_copy(0, 0).start()

    idx_copy(i, slot).wait()

    @pl.when(i + 1 < n_tiles)
    def _():
        idx_copy(i + 1, 1 - slot).start()

    for r in range(tm):
        for k in range(2):
            d = dsm[slot, k * tm + r]
            _row_copy(_token_rows(x_ref, r), _token_rows(xs_hbm, d), sem_d).start(priority=k)
    for _ in range(2):
        _row_copy(x_ref, _token_rows(xs_hbm, 0, tm), sem_d).wait()


def _dispatch(x_rows, dest_tiles, n_slots):
    n_tiles = dest_tiles.shape[0]
    tm = dest_tiles.shape[2] // 2
    zeros = jnp.zeros((n_slots * ROW_TILE, LANES), u32)
    return pl.pallas_call(
        functools.partial(_dispatch_kernel, tm=tm, n_tiles=n_tiles),
        grid=(n_tiles,),
        in_specs=[pl.BlockSpec(memory_space=pl.ANY),
                  pl.BlockSpec((tm * ROW_TILE, LANES), lambda i: (i, 0)),
                  pl.BlockSpec(memory_space=pl.ANY)],
        out_specs=pl.BlockSpec(memory_space=pl.ANY),
        out_shape=jax.ShapeDtypeStruct(zeros.shape, u32),
        scratch_shapes=[pltpu.SMEM((2, 2 * tm), i32), pltpu.SemaphoreType.DMA((2,)), pltpu.SemaphoreType.DMA(())],
        input_output_aliases={2: 0},
        compiler_params=_cparams(("arbitrary",)),
    )(dest_tiles, x_rows, zeros)


def _ffn_kernel(be_ref, na_ref, xs_ref, wg_ref, wu_ref, wd_ref, y_ref):
    del be_ref
    blk = pl.program_id(0)

    @pl.when(blk < na_ref[0])
    def _():
        parts = [_unpack_pair(w) for w in _load_token_rows(xs_ref, MOE_BLK)]
        lo = jnp.concatenate([p[0].astype(bf16) for p in parts], axis=1)
        hi = jnp.concatenate([p[1].astype(bf16) for p in parts], axis=1)
        half = lo.shape[1]
        gate = _dot(lo, wg_ref[0, :half, :]) + _dot(hi, wg_ref[0, half:, :])
        up = _dot(lo, wu_ref[0, :half, :]) + _dot(hi, wu_ref[0, half:, :])
        hid = (gate * jax.nn.sigmoid(gate) * up).astype(bf16)
        out = _dot(hid, wd_ref[0])
        _store_token_rows(y_ref, _pack_pair(out[:, :half], out[:, half:]))

    @pl.when(blk >= na_ref[0])
    def _():
        y_ref[...] = jnp.zeros_like(y_ref)


def _expert_ffn(xs, block_e, n_active, wg, wu, wd):
    d, dff = wg.shape[1], wg.shape[2]
    n_blocks = xs.shape[0] // (MOE_BLK * ROW_TILE)
    grid_spec = pltpu.PrefetchScalarGridSpec(
        num_scalar_prefetch=2,
        grid=(n_blocks,),
        in_specs=[pl.BlockSpec((MOE_BLK * ROW_TILE, LANES), lambda b, be, na: (b, 0)),
                  pl.BlockSpec((1, d, dff), lambda b, be, na: (be[b], 0, 0)),
                  pl.BlockSpec((1, d, dff), lambda b, be, na: (be[b], 0, 0)),
                  pl.BlockSpec((1, dff, d), lambda b, be, na: (be[b], 0, 0))],
        out_specs=pl.BlockSpec((MOE_BLK * ROW_TILE, LANES), lambda b, be, na: (b, 0)),
    )
    return pl.pallas_call(
        _ffn_kernel,
        grid_spec=grid_spec,
        out_shape=jax.ShapeDtypeStruct(xs.shape, u32),
        compiler_params=_cparams(("arbitrary",)),
    )(block_e, n_active, xs, wg, wu, wd)


def _combine_kernel(first_hbm, pairs_hbm, x1_ref, tw_ref, g_ref, b_ref, y_hbm, h32_ref, h16_ref,
                    ism, buf_a, buf_b, sem_i, sem_a, sem_b, *, tm, n_steps, alpha):
    j = pl.program_id(0)
    s = j % 2

    def pair_copy(step, sl):
        return _row_copy(pairs_hbm.at[step], ism.at[pl.ds(sl, 1)], sem_i.at[sl])

    def issue(buf, sem, sl, off):
        for r in range(tm):
            for k in range(2):
                d = ism[sl, off + k * tm + r]
                _row_copy(_token_rows(y_hbm, d), _token_rows(buf.at[k], r), sem).start(priority=k)

    def wait_buf(buf, sem):
        for k in range(2):
            _row_copy(_token_rows(y_hbm, 0, tm), buf.at[k], sem).wait()

    def finish(buf, t0):
        rs = slice(t0, t0 + tm)
        w0, w1 = tw_ref[rs, 0:1], tw_ref[rs, 1:2]
        rows0 = _load_token_rows(buf.at[0], tm)
        rows1 = _load_token_rows(buf.at[1], tm)
        half = ROW_TILE * LANES
        res = [None] * (2 * ROW_TILE)
        for c in range(ROW_TILE):
            lo0, hi0 = _unpack_pair(rows0[c])
            lo1, hi1 = _unpack_pair(rows1[c])
            res[c] = alpha * x1_ref[rs, c * LANES:(c + 1) * LANES] + (w0 * lo0 + w1 * lo1)
            res[ROW_TILE + c] = (alpha * x1_ref[rs, half + c * LANES:half + (c + 1) * LANES]
                                 + (w0 * hi0 + w1 * hi1))
        inv_d = 1.0 / (2 * half)
        mu = jnp.sum(functools.reduce(jnp.add, res), axis=-1, keepdims=True) * inv_d
        res = [r - mu for r in res]
        var = jnp.sum(functools.reduce(jnp.add, [r * r for r in res]), axis=-1, keepdims=True) * inv_d
        rstd = lax.rsqrt(var + LN_EPS)
        for c, r in enumerate(res):
            cs = slice(c * LANES, (c + 1) * LANES)
            o = r * rstd * g_ref[:, cs] + b_ref[:, cs]
            h32_ref[rs, cs] = o
            h16_ref[rs, cs] = o.astype(bf16)

    @pl.when(j == 0)
    def _():
        first = _row_copy(first_hbm.at[0], ism.at[pl.ds(0, 1), pl.ds(0, 2 * tm)], sem_i.at[0])
        first.start()
        first.wait()

        def body(r, carry):
            for k in range(2):
                d = ism[0, k * tm + r]
                _row_copy(_token_rows(y_hbm, d), _token_rows(buf_a.at[k], r), sem_a).start(priority=k)
            return carry

        lax.fori_loop(0, tm, body, 0, unroll=8)
        pair_copy(0, 0).start()

    pair_copy(j, s).wait()

    @pl.when(j + 1 < n_steps)
    def _():
        pair_copy(j + 1, 1 - s).start()

    wait_buf(buf_a, sem_a)
    issue(buf_b, sem_b, s, 0)
    finish(buf_a, 0)
    wait_buf(buf_b, sem_b)
    issue(buf_a, sem_a, s, 2 * tm)
    finish(buf_b, tm)

    @pl.when(j == n_steps - 1)
    def _():
        wait_buf(buf_a, sem_a)


def _combine(dest_tiles, x1, tw_t, ln_g, ln_b, y, alpha):
    m, d = x1.shape
    n_tiles = dest_tiles.shape[0]
    tm = dest_tiles.shape[2] // 2
    assert n_tiles % 2 == 0
    n_steps = n_tiles // 2
    nxt = jnp.minimum(2 * jnp.arange(n_steps) + 2, n_tiles - 1)
    pairs = jnp.concatenate([dest_tiles[1::2], dest_tiles[nxt]], axis=2)
    row = lambda i: (i, 0)
    fix = lambda i: (0, 0)
    half_buf = pltpu.VMEM((2, tm * ROW_TILE, LANES), u32)
    return pl.pallas_call(
        functools.partial(_combine_kernel, tm=tm, n_steps=n_steps, alpha=alpha),
        grid=(n_steps,),
        in_specs=[pl.BlockSpec(memory_space=pl.ANY), pl.BlockSpec(memory_space=pl.ANY),
                  pl.BlockSpec((2 * tm, d), row),
                  pl.BlockSpec((2 * tm, 2), row),
                  pl.BlockSpec((1, d), fix), pl.BlockSpec((1, d), fix),
                  pl.BlockSpec(memory_space=pl.ANY)],
        out_specs=[pl.BlockSpec((2 * tm, d), row), pl.BlockSpec((2 * tm, d), row)],
        out_shape=[jax.ShapeDtypeStruct((m, d), f32), jax.ShapeDtypeStruct((m, d), bf16)],
        scratch_shapes=[pltpu.SMEM((2, 4 * tm), i32), half_buf, half_buf,
                        pltpu.SemaphoreType.DMA((2,)), pltpu.SemaphoreType.DMA(()), pltpu.SemaphoreType.DMA(())],
        compiler_params=_cparams(("arbitrary",)),
    )(dest_tiles, pairs, x1, tw_t, ln_g, ln_b, y)


def _moe_plan(top_e, n_tok):
    rank, counts = _expert_ranks(top_e.reshape(1, 2 * n_tok))
    counts = counts.reshape(N_EXPERTS).astype(i32)
    padded = (counts + MOE_BLK - 1) // MOE_BLK * MOE_BLK
    pend = jnp.cumsum(padded)
    pstart = pend - padded
    e_flat = top_e.reshape(2 * n_tok)
    start_of = jnp.sum(jnp.where(e_flat[:, None] == jnp.arange(N_EXPERTS, dtype=i32)[None, :],
                                 pstart[None, :], 0), axis=1)
    dest = (start_of + rank.reshape(2 * n_tok)).reshape(2, n_tok)
    n_blocks = -(-(2 * n_tok) // MOE_BLK) + N_EXPERTS
    blk_row = jnp.arange(n_blocks, dtype=i32) * MOE_BLK
    block_e = jnp.minimum(jnp.sum((blk_row[:, None] >= pend[None, :]).astype(i32), axis=1), N_EXPERTS - 1)
    n_active = (pend[-1] // MOE_BLK).reshape(1).astype(i32)
    tm = _pick(n_tok, (TOK_TILE, 128))
    dest_tiles = dest.reshape(2, n_tok // tm, tm).transpose(1, 0, 2).reshape(n_tok // tm, 1, 2 * tm)
    return dest_tiles, block_e.astype(i32), n_active, n_blocks * MOE_BLK


def kernel(x, meta_tokens, w_in, w_gate_lr, b_gate_lr, sb_out_norm, gla_out_norm, w_out, ln1_g, ln1_b,
           w_router, b_router, w_ff_gate, w_ff_up, w_ff_down, ln2_g, ln2_b):
    bsz, seq, d = x.shape
    depth = w_in.shape[0]
    alpha = float((2 * depth) ** 0.25)
    length = N_META + seq
    lp = -(-length // SEQ_ALIGN) * SEQ_ALIGN
    pad = lp - length
    assert (pad - (-length) % GLA_CHUNK) % GLA_CHUNK == 0
    n_tok = bsz * lp

    meta = jnp.broadcast_to(meta_tokens[None].astype(x.dtype), (bsz, N_META, d))
    h = jnp.concatenate([jnp.zeros((bsz, pad, d), x.dtype), meta, x], axis=1).reshape(n_tok, d)
    h16 = h.astype(bf16)

    sb_w = SB_HEADS * SB_HEAD_DIM
    gla_kw = GLA_HEADS * GLA_DK
    gla_vw = GLA_HEADS * GLA_DV
    o_lr = 3 * sb_w + 2 * gla_kw + gla_vw
    o_go = o_lr + GLA_GATE_RANK
    w_main = jnp.concatenate([w_in[:, :, :o_lr], w_in[:, :, o_go:]], axis=2).astype(bf16)
    w_lr = jnp.pad(w_in[:, :, o_lr:o_go], ((0, 0), (0, 0), (0, LANES - GLA_GATE_RANK))).astype(bf16)
    w_g = jnp.pad(w_gate_lr, ((0, 0), (0, LANES - GLA_GATE_RANK), (0, 0))).astype(bf16)
    w_out16 = w_out.astype(bf16)
    wr_hi = w_router.astype(bf16)
    wr_lo = (w_router - wr_hi.astype(f32)).astype(bf16)
    wr = jnp.concatenate([jnp.pad(jnp.concatenate([wr_hi, wr_lo], axis=1), ((0, 0), (0, LANES - 2 * N_EXPERTS))),
                          jnp.pad(wr_hi, ((0, 0), (0, LANES - N_EXPERTS)))], axis=0)
    br = b_router.astype(f32).reshape(N_EXPERTS, 1)
    wg16, wu16, wd16 = w_ff_gate.astype(bf16), w_ff_up.astype(bf16), w_ff_down.astype(bf16)

    for l in range(depth):
        proj = _matmul(h16, w_main[l], bf16)
        log_a = _gla_gate(h16, w_lr[l], w_g[l], b_gate_lr[l].reshape(1, gla_kw))
        proj3 = proj.reshape(bsz, lp, proj.shape[1])
        y_sb = _sb_attention(proj3, sb_out_norm[l].reshape(1, sb_w), pad)
        y_gla = _gla(proj3, log_a.reshape(bsz, lp, gla_kw), gla_out_norm[l].reshape(1, gla_vw), pad)
        x1, x1p, top_e, top_w = _oproj_ln_route(
            y_sb.reshape(n_tok, sb_w), y_gla.reshape(n_tok, gla_vw), w_out16[l, :sb_w], w_out16[l, sb_w:],
            h, ln1_g[l].reshape(1, d), ln1_b[l].reshape(1, d), wr, br, alpha)
        dest_tiles, block_e, n_active, n_rows = _moe_plan(top_e, n_tok)
        xs = _dispatch(x1p, dest_tiles, n_rows)
        y = _expert_ffn(xs, block_e, n_active, wg16[l], wu16[l], wd16[l])
        h, h16 = _combine(dest_tiles, x1, top_w.T, ln2_g[l].reshape(1, d), ln2_b[l].reshape(1, d), y, alpha)

    return h.reshape(bsz, lp, d)[:, pad + N_META:, :]
```

```python
import functools
import math

import jax
import jax.numpy as jnp
from jax import lax
from jax.experimental import pallas as pl
from jax.experimental.pallas import tpu as pltpu

f32 = jnp.float32
bf16 = jnp.bfloat16
u32 = jnp.uint32
i32 = jnp.int32

N_META = 16
SB_HEADS = 8
SB_HEAD_DIM = 128
GLA_HEADS = 4
GLA_DK = 128
GLA_DV = 256
GLA_GATE_RANK = 16
GLA_GATE_TAU = 16.0
GLA_CHUNK = 64
N_EXPERTS = 16
N_GROUPS = 4
EXPERTS_PER_GROUP = 4
LN_EPS = 1e-5
RMS_EPS = 1e-6

LANES = 128
VMEM_LIMIT = 56 * 1024 * 1024

SEQ_ALIGN = 256
SB_TQ = 128
SB_WINDOW = 3
SB_HEADS_PER_STEP = 8
GLA_ROWS = 256
OPROJ_ROWS = 128
MOE_BLK = 512
TOK_TILE = 256
RANK_CHUNK = 512
SB_DEAD_LOGW = 104.0


def _pick(n, cands):
    for c in cands:
        if n % c == 0:
            return c
    raise ValueError(f"no tile in {cands} divides {n}")


def _cparams(sem, vmem=VMEM_LIMIT):
    return pltpu.CompilerParams(dimension_semantics=sem, vmem_limit_bytes=vmem)


def _dot(a, b):
    return jnp.dot(a, b, preferred_element_type=f32)


def _dot_nt(a, b):
    return lax.dot_general(a, b, (((1,), (1,)), ((), ())), preferred_element_type=f32)


def _dot_tn(a, b):
    return lax.dot_general(a, b, (((0,), (0,)), ((), ())), preferred_element_type=f32)


def _pack_pair(lo, hi):
    lo_b = lax.bitcast_convert_type(lo.astype(bf16).astype(f32), u32)
    hi_b = lax.bitcast_convert_type(hi.astype(bf16).astype(f32), u32)
    return (hi_b & jnp.uint32(0xFFFF0000)) | (lo_b >> 16)


def _unpack_pair(w):
    lo = lax.bitcast_convert_type(w << 16, f32)
    hi = lax.bitcast_convert_type(w & jnp.uint32(0xFFFF0000), f32)
    return lo, hi


ROW_TILE = 8


def _store_token_rows(ref, words, t0=0):
    n = words.shape[0]
    for s in range(ROW_TILE):
        ref[pl.ds(t0 * ROW_TILE + s, n, stride=ROW_TILE), :] = words[:, s * LANES:(s + 1) * LANES]


def _load_token_rows(ref, n):
    return [ref[pl.ds(s, n, stride=ROW_TILE), :] for s in range(ROW_TILE)]


def _mm_kernel(x_ref, w_ref, o_ref):
    o_ref[...] = _dot(x_ref[...], w_ref[...]).astype(o_ref.dtype)


def _matmul(x, w, layer, out_dtype):
    m, k = x.shape
    n = w.shape[2]
    tm = _pick(m, (2048, 1024, 512, 256, 128))
    tn = _pick(n, (512, 256, 128))
    return pl.pallas_call(
        _mm_kernel,
        grid=(m // tm, n // tn),
        in_specs=[pl.BlockSpec((tm, k), lambda i, j: (i, 0)),
                  pl.BlockSpec((None, k, tn), lambda i, j: (layer, 0, j))],
        out_specs=pl.BlockSpec((tm, tn), lambda i, j: (i, j)),
        out_shape=jax.ShapeDtypeStruct((m, n), out_dtype),
        compiler_params=_cparams(("parallel", "parallel")),
    )(x, w)


def _gate_kernel(x_ref, wlr_ref, wg_ref, b_ref, o_ref):
    g_lr = _dot(x_ref[...], wlr_ref[...]).astype(bf16)
    y = _dot(g_lr, wg_ref[...]) + b_ref[...]
    log_sig = jnp.minimum(y, 0.0) - jnp.log(1.0 + jnp.exp(-jnp.abs(y)))
    o_ref[...] = log_sig * (1.0 / GLA_GATE_TAU)


def _gla_gate(x, w_lr, w_g, b_g):
    m, k = x.shape
    n = w_g.shape[1]
    tm = _pick(m, (512, 256, 128))
    return pl.pallas_call(
        _gate_kernel,
        grid=(m // tm,),
        in_specs=[pl.BlockSpec((tm, k), lambda i: (i, 0)),
                  pl.BlockSpec((k, LANES), lambda i: (0, 0)),
                  pl.BlockSpec((LANES, n), lambda i: (0, 0)),
                  pl.BlockSpec((1, n), lambda i: (0, 0))],
        out_specs=pl.BlockSpec((tm, n), lambda i: (i, 0)),
        out_shape=jax.ShapeDtypeStruct((m, n), f32),
        compiler_params=_cparams(("parallel",)),
    )(x, w_lr, w_g, b_g)


def _sb_blocks(q_ref, k_ref, v_ref, tri, heads, k0, n_blocks, valid, carries):
    tq, d = SB_TQ, SB_HEAD_DIM
    scale = d ** -0.5
    span = pl.ds(k0, n_blocks * tq)
    cols = [slice(h * d, (h + 1) * d) for h in range(heads)]
    z = [_dot_nt(q_ref[0, :, cs], k_ref[0, span, cs]) * scale for cs in cols]
    log_beta, sums = [], []
    for h in range(heads):
        for j in range(n_blocks):
            zj = z[h][:, j * tq:(j + 1) * tq]
            sp = jnp.maximum(zj, 0.0) + jnp.log(1.0 + jnp.exp(-jnp.abs(zj)))
            log_beta.append(zj - sp)
            sums.append(_dot(jnp.where(valid[j], sp, 0.0).astype(bf16), tri))
    outs, new_carries = [], []
    for h in range(heads):
        carry = carries[h]
        w = [None] * n_blocks
        for j in reversed(range(n_blocks)):
            sm = sums[h * n_blocks + j]
            between = sm[:, :tq] + carry
            w[j] = jnp.where(valid[j], jnp.exp(log_beta[h * n_blocks + j] - between), 0.0).astype(bf16)
            carry = carry + sm[:, tq:]
        outs.append(_dot(w[0] if n_blocks == 1 else jnp.concatenate(w, axis=1), v_ref[0, span, cols[h]]))
        new_carries.append(carry)
    return outs, new_carries


def _sb_kernel(q_ref, k_ref, v_ref, tri_ref, g_ref, o_ref, *, pad, heads):
    tq, d = SB_TQ, SB_HEAD_DIM
    i = pl.program_id(2)
    t0 = i * tq
    wb = jnp.maximum(i - (SB_WINDOW - 1), 0)
    ws = pl.multiple_of(wb * tq, tq)
    key_minus_query = lax.broadcasted_iota(i32, (tq, tq), 1) - lax.broadcasted_iota(i32, (tq, tq), 0)
    col = lax.broadcasted_iota(i32, (tq, tq), 1)
    tri = tri_ref[...]

    def valid_for(k0):
        return jnp.logical_and(key_minus_query < t0 - k0, col >= pad - k0)

    def lowest(carries):
        m = jnp.min(carries[0])
        for c in carries[1:]:
            m = jnp.minimum(m, jnp.min(c))
        return m

    zero = [jnp.zeros((tq, tq), f32)] * heads
    accs, carries = _sb_blocks(q_ref, k_ref, v_ref, tri, heads, ws, SB_WINDOW,
                               [valid_for(ws + j * tq) for j in range(SB_WINDOW)], zero)

    def cond(c):
        jb, cmin, _, _ = c
        return jnp.logical_and(jb >= 0, cmin < SB_DEAD_LOGW)

    def body(c):
        jb, _, accs, carries = c
        k0 = pl.multiple_of(jb * tq, tq)
        outs, carries = _sb_blocks(q_ref, k_ref, v_ref, tri, heads, k0, 1, [valid_for(k0)], list(carries))
        return jb - 1, lowest(carries), tuple(a + o for a, o in zip(accs, outs)), tuple(carries)

    _, _, accs, _ = lax.while_loop(cond, body, (wb - 1, lowest(carries), tuple(accs), tuple(carries)))
    for h in range(heads):
        cs = slice(h * d, (h + 1) * d)
        ms = jnp.mean(accs[h] * accs[h], axis=1, keepdims=True)
        o_ref[0, :, cs] = (accs[h] * lax.rsqrt(ms + RMS_EPS) * g_ref[:, cs]).astype(o_ref.dtype)


def _sb_attention(proj, norm_g, pad):
    b, lp, _ = proj.shape
    tq, heads = SB_TQ, SB_HEADS_PER_STEP
    wid = heads * SB_HEAD_DIM
    groups = SB_HEADS // heads
    later = jnp.arange(tq)[:, None] > jnp.arange(tq)[None, :]
    tri = jnp.concatenate([later, jnp.ones((tq, tq), bool)], axis=1).astype(bf16)
    return pl.pallas_call(
        functools.partial(_sb_kernel, pad=pad, heads=heads),
        grid=(b, groups, lp // tq),
        in_specs=[pl.BlockSpec((1, tq, wid), lambda bi, h, i: (bi, i, h)),
                  pl.BlockSpec((1, lp, wid), lambda bi, h, i: (bi, 0, groups + h)),
                  pl.BlockSpec((1, lp, wid), lambda bi, h, i: (bi, 0, 2 * groups + h)),
                  pl.BlockSpec((tq, 2 * tq), lambda bi, h, i: (0, 0)),
                  pl.BlockSpec((1, wid), lambda bi, h, i: (0, h))],
        out_specs=pl.BlockSpec((1, tq, wid), lambda bi, h, i: (bi, i, h)),
        out_shape=jax.ShapeDtypeStruct((b, lp, SB_HEADS * SB_HEAD_DIM), bf16),
        compiler_params=_cparams(("parallel", "parallel", "parallel")),
    )(proj, proj, proj, tri, norm_g)


def _gla_kernel(q_ref, k_ref, v_ref, la_ref, go_ref, gn_ref, o_ref, st_ref, *, pad):
    c, rows, dk, dv = GLA_CHUNK, GLA_ROWS, GLA_DK, GLA_DV
    s = pl.program_id(1)

    @pl.when(s == 0)
    def _():
        st_ref[...] = jnp.zeros_like(st_ref)

    real = (s * rows + lax.broadcasted_iota(i32, (rows, 1), 0)) >= pad
    r_i = lax.broadcasted_iota(i32, (rows, rows), 0)
    c_i = lax.broadcasted_iota(i32, (rows, rows), 1)
    shift = c.bit_length() - 1
    causal = jnp.logical_and(c_i <= r_i, (r_i >> shift) == (c_i >> shift))
    causal_b = causal.astype(bf16)
    scale = dk ** -0.5

    for h in range(GLA_HEADS):
        ks = slice(h * dk, (h + 1) * dk)
        vs = slice(h * dv, (h + 1) * dv)
        g = jnp.where(real, la_ref[0, :, ks], 0.0)
        g_hi = g.astype(bf16)
        g_lo = (g - g_hi.astype(f32)).astype(bf16)
        bcum = _dot(causal_b, g_hi) + _dot(causal_b, g_lo)
        qf = q_ref[0, :, ks].astype(f32) * scale
        kf = jnp.where(real, k_ref[0, :, ks].astype(f32), 0.0)
        vb = v_ref[0, :, vs]
        q_dec = (qf * jnp.exp(bcum)).astype(bf16)
        k_inv = (kf * jnp.exp(-bcum)).astype(bf16)
        scores = jnp.where(causal, _dot_nt(q_dec, k_inv), 0.0)
        o_intra = _dot(scores.astype(bf16), vb)
        st = st_ref[h]
        o_inter = []
        for n in range(rows // c):
            rs = slice(n * c, (n + 1) * c)
            b_last = bcum[(n + 1) * c - 1:(n + 1) * c, :]
            k_dec = (kf[rs] * jnp.exp(b_last - bcum[rs])).astype(bf16)
            o_inter.append(_dot_nt(q_dec[rs], st.astype(bf16)))
            st = st * jnp.exp(b_last) + _dot_tn(vb[rs], k_dec)
        st_ref[h] = st
        o = o_intra + jnp.concatenate(o_inter, axis=0)
        ms = jnp.mean(o * o, axis=1, keepdims=True)
        gate = go_ref[0, :, vs].astype(f32)
        y = o * lax.rsqrt(ms + RMS_EPS) * gn_ref[:, vs] * (gate * jax.nn.sigmoid(gate))
        o_ref[0, :, vs] = y.astype(o_ref.dtype)


def _gla(proj, log_a, norm_g, pad):
    b, lp, _ = proj.shape
    rows = GLA_ROWS
    kw, vw = GLA_HEADS * GLA_DK, GLA_HEADS * GLA_DV
    return pl.pallas_call(
        functools.partial(_gla_kernel, pad=pad),
        grid=(b, lp // rows),
        in_specs=[pl.BlockSpec((1, rows, kw), lambda bi, s: (bi, s, 6)),
                  pl.BlockSpec((1, rows, kw), lambda bi, s: (bi, s, 7)),
                  pl.BlockSpec((1, rows, vw), lambda bi, s: (bi, s, 4)),
                  pl.BlockSpec((1, rows, kw), lambda bi, s: (bi, s, 0)),
                  pl.BlockSpec((1, rows, vw), lambda bi, s: (bi, s, 5)),
                  pl.BlockSpec((1, vw), lambda bi, s: (0, 0))],
        out_specs=pl.BlockSpec((1, rows, vw), lambda bi, s: (bi, s, 0)),
        out_shape=jax.ShapeDtypeStruct((b, lp, vw), bf16),
        scratch_shapes=[pltpu.VMEM((GLA_HEADS, GLA_DV, GLA_DK), f32)],
        compiler_params=_cparams(("parallel", "arbitrary")),
    )(proj, proj, proj, log_a, proj, norm_g)


def _layer_norm_rows(r, g, b):
    mu = jnp.mean(r, axis=-1, keepdims=True)
    d = r - mu
    var = jnp.mean(d * d, axis=-1, keepdims=True)
    return d * lax.rsqrt(var + LN_EPS) * g + b


def _route(logits):
    m = jnp.max(logits, axis=0, keepdims=True)
    e = jnp.exp(logits - m)
    aff = e / jnp.sum(e, axis=0, keepdims=True)
    rows = [aff[j:j + 1, :] for j in range(N_EXPERTS)]
    scores = []
    for g in range(N_GROUPS):
        a, b, c, d = rows[4 * g:4 * g + 4]
        hi1, lo1 = jnp.maximum(a, b), jnp.minimum(a, b)
        hi2, lo2 = jnp.maximum(c, d), jnp.minimum(c, d)
        scores.append(jnp.maximum(hi1, hi2) + jnp.maximum(jnp.minimum(hi1, hi2), jnp.maximum(lo1, lo2)))
    best = scores[0]
    g_sel = jnp.zeros(best.shape, i32)
    for g in range(1, N_GROUPS):
        better = scores[g] > best
        g_sel = jnp.where(better, g, g_sel)
        best = jnp.where(better, scores[g], best)
    sel = list(rows[0:4])
    for g in range(1, N_GROUPS):
        here = g_sel == g
        sel = [jnp.where(here, rows[4 * g + j], sel[j]) for j in range(4)]
    v1, i1 = sel[0], jnp.zeros(best.shape, i32)
    for j in range(1, 4):
        gt = sel[j] > v1
        i1 = jnp.where(gt, j, i1)
        v1 = jnp.where(gt, sel[j], v1)
    v2, i2 = jnp.full(best.shape, -1.0, f32), jnp.zeros(best.shape, i32)
    for j in range(4):
        cand = jnp.where(i1 == j, -1.0, sel[j])
        gt = cand > v2
        i2 = jnp.where(gt, j, i2)
        v2 = jnp.where(gt, cand, v2)
    s = v1 + v2
    top_e = jnp.concatenate([4 * g_sel + i1, 4 * g_sel + i2], axis=0)
    top_w = jnp.concatenate([v1 / s, v2 / s], axis=0)
    return top_e, top_w


def _oproj_kernel(ysb_ref, ygla_ref, w1_ref, w2_ref, h_ref, g_ref, b_ref, wr_ref, br_ref,
                  x1_ref, x1p_ref, te_ref, tw_ref, *, alpha):
    tm, d = h_ref.shape
    half = d // 2
    n = tm // OPROJ_ROWS
    groups = [slice(p * OPROJ_ROWS, (p + 1) * OPROJ_ROWS) for p in range(n)]
    mixes = [_dot(ysb_ref[rs, :], w1_ref[...]) + _dot(ygla_ref[rs, :], w2_ref[...]) for rs in groups]
    wr = wr_ref[...]
    for p, rs in enumerate(groups):
        x1 = _layer_norm_rows(alpha * h_ref[rs, :] + mixes[p], g_ref[...], b_ref[...])
        x1_ref[rs, :] = x1
        _store_token_rows(x1p_ref, _pack_pair(x1[:, :half], x1[:, half:]), p * OPROJ_ROWS)
        xh = x1.astype(bf16)
        xl = (x1 - xh.astype(f32)).astype(bf16)
        l_t = (_dot(xh, wr[:d]) + _dot(xl, wr[d:])).T
        logits = l_t[:N_EXPERTS] + l_t[N_EXPERTS:2 * N_EXPERTS] + br_ref[...]
        top_e, top_w = _route(logits)
        te_ref[:, rs] = top_e
        tw_ref[:, rs] = top_w


def _oproj_ln_route(y_sb, y_gla, w_out, layer, h, ln_g, ln_b, wr, br, alpha):
    m, d = h.shape
    kh = y_sb.shape[1]
    tm = _pick(m, (512, 256, 128))
    row = lambda i: (i, 0)
    fix = lambda i: (0, 0)
    col = lambda i: (0, i)
    return pl.pallas_call(
        functools.partial(_oproj_kernel, alpha=alpha),
        grid=(m // tm,),
        in_specs=[pl.BlockSpec((tm, kh), row), pl.BlockSpec((tm, kh), row),
                  pl.BlockSpec((None, kh, d), lambda i: (layer, 0, 0)),
                  pl.BlockSpec((None, kh, d), lambda i: (layer, 1, 0)),
                  pl.BlockSpec((tm, d), row),
                  pl.BlockSpec((1, d), fix), pl.BlockSpec((1, d), fix),
                  pl.BlockSpec((2 * d, LANES), fix), pl.BlockSpec((N_EXPERTS, 1), fix)],
        out_specs=[pl.BlockSpec((tm, d), row), pl.BlockSpec((tm * ROW_TILE, LANES), row),
                   pl.BlockSpec((2, tm), col), pl.BlockSpec((2, tm), col)],
        out_shape=[jax.ShapeDtypeStruct((m, d), f32), jax.ShapeDtypeStruct((m * ROW_TILE, LANES), u32),
                   jax.ShapeDtypeStruct((2, m), i32), jax.ShapeDtypeStruct((2, m), f32)],
        compiler_params=_cparams(("parallel",)),
    )(y_sb, y_gla, w_out, w_out, h, ln_g, ln_b, wr, br)


def _rank_kernel(e_ref, rank_ref, cnt_ref, carry_ref, *, chunk):
    @pl.when(pl.program_id(0) == 0)
    def _():
        carry_ref[...] = jnp.zeros_like(carry_ref)

    e = e_ref[...]
    onehot = lax.broadcasted_iota(i32, (N_EXPERTS, chunk), 0) == e
    before = (lax.broadcasted_iota(i32, (chunk, chunk), 0)
              < lax.broadcasted_iota(i32, (chunk, chunk), 1)).astype(bf16)
    seen = _dot(onehot.astype(bf16), before) + carry_ref[...]
    rank_ref[...] = jnp.sum(jnp.where(onehot, seen, 0.0), axis=0, keepdims=True).astype(i32)
    total = carry_ref[...] + jnp.sum(onehot.astype(f32), axis=1, keepdims=True)
    carry_ref[...] = total
    cnt_ref[...] = total


def _expert_ranks(e_flat):
    n = e_flat.shape[1]
    chunk = _pick(n, (RANK_CHUNK, 256, 128))
    return pl.pallas_call(
        functools.partial(_rank_kernel, chunk=chunk),
        grid=(n // chunk,),
        in_specs=[pl.BlockSpec((1, chunk), lambda i: (0, i))],
        out_specs=[pl.BlockSpec((1, chunk), lambda i: (0, i)),
                   pl.BlockSpec((N_EXPERTS, 1), lambda i: (0, 0))],
        out_shape=[jax.ShapeDtypeStruct((1, n), i32), jax.ShapeDtypeStruct((N_EXPERTS, 1), f32)],
        scratch_shapes=[pltpu.VMEM((N_EXPERTS, 1), f32)],
        compiler_params=_cparams(("arbitrary",)),
    )(e_flat)


def _row_copy(src, dst, sem):
    return pltpu.make_async_copy(src, dst, sem)


def _token_rows(ref, t, n=1):
    start = t * ROW_TILE
    if not isinstance(start, int):
        start = pl.multiple_of(start, ROW_TILE)
    return ref.at[pl.ds(start, n * ROW_TILE)]


ZERO_CHUNK = 64


def _dispatch_kernel(pads_ref, dest_hbm, x_ref, xs_hbm, dsm, zbuf, sem_i, sem_d, sem_z, *, tm, n_tiles):
    i = pl.program_id(0)
    slot = i % 2

    def idx_copy(t, sl):
        return _row_copy(dest_hbm.at[t], dsm.at[pl.ds(sl, 1)], sem_i.at[sl])

    def zero_copy(first, n):
        return _row_copy(_token_rows(zbuf, 0, n), _token_rows(xs_hbm, first, n), sem_z)

    @pl.when(i == 0)
    def _():
        idx_copy(0, 0).start()
        zbuf[...] = jnp.zeros_like(zbuf)
        n_big, n_one = 0, 0
        n_runs = pads_ref.shape[0] // 2
        for e in range(n_runs):
            first, count = pads_ref[e], pads_ref[n_runs + e]
            big = count >> (ZERO_CHUNK.bit_length() - 1)
            rest = count - big * ZERO_CHUNK

            def chunk(c, carry, first=first):
                zero_copy(first + c * ZERO_CHUNK, ZERO_CHUNK).start()
                return carry

            def single(r, carry, first=first + big * ZERO_CHUNK):
                zero_copy(first + r, 1).start()
                return carry

            lax.fori_loop(0, big, chunk, 0)
            lax.fori_loop(0, rest, single, 0)
            n_big, n_one = n_big + big, n_one + rest
        lax.fori_loop(0, n_big, lambda c, carry: (zero_copy(0, ZERO_CHUNK).wait(), carry)[1], 0)
        lax.fori_loop(0, n_one, lambda c, carry: (zero_copy(0, 1).wait(), carry)[1], 0)

    idx_copy(i, slot).wait()

    @pl.when(i + 1 < n_tiles)
    def _():
        idx_copy(i + 1, 1 - slot).start()

    for r in range(tm):
        for k in range(2):
            d = dsm[slot, k * tm + r]
            _row_copy(_token_rows(x_ref, r), _token_rows(xs_hbm, d), sem_d).start(priority=k)
    for _ in range(2):
        _row_copy(x_ref, _token_rows(xs_hbm, 0, tm), sem_d).wait()


def _dispatch(x_rows, dest_tiles, pads, n_slots):
    n_tiles = dest_tiles.shape[0]
    tm = dest_tiles.shape[2] // 2
    grid_spec = pltpu.PrefetchScalarGridSpec(
        num_scalar_prefetch=1,
        grid=(n_tiles,),
        in_specs=[pl.BlockSpec(memory_space=pl.ANY),
                  pl.BlockSpec((tm * ROW_TILE, LANES), lambda i, pads: (i, 0))],
        out_specs=pl.BlockSpec(memory_space=pl.ANY),
        scratch_shapes=[pltpu.SMEM((2, 2 * tm), i32), pltpu.VMEM((ZERO_CHUNK * ROW_TILE, LANES), u32),
                        pltpu.SemaphoreType.DMA((2,)), pltpu.SemaphoreType.DMA(()), pltpu.SemaphoreType.DMA(())],
    )
    return pl.pallas_call(
        functools.partial(_dispatch_kernel, tm=tm, n_tiles=n_tiles),
        grid_spec=grid_spec,
        out_shape=jax.ShapeDtypeStruct((n_slots * ROW_TILE, LANES), u32),
        compiler_params=_cparams(("arbitrary",)),
    )(pads, dest_tiles, x_rows)


def _ffn_kernel(be_ref, na_ref, xs_ref, wg_ref, wu_ref, wd_ref, y_ref):
    del be_ref
    blk = pl.program_id(0)

    @pl.when(blk < na_ref[0])
    def _():
        parts = [_unpack_pair(w) for w in _load_token_rows(xs_ref, MOE_BLK)]
        lo = jnp.concatenate([p[0].astype(bf16) for p in parts], axis=1)
        hi = jnp.concatenate([p[1].astype(bf16) for p in parts], axis=1)
        half = lo.shape[1]
        gate = _dot(lo, wg_ref[0, :half, :]) + _dot(hi, wg_ref[0, half:, :])
        up = _dot(lo, wu_ref[0, :half, :]) + _dot(hi, wu_ref[0, half:, :])
        hid = (gate * jax.nn.sigmoid(gate) * up).astype(bf16)
        out = _dot(hid, wd_ref[0])
        _store_token_rows(y_ref, _pack_pair(out[:, :half], out[:, half:]))

    @pl.when(blk >= na_ref[0])
    def _():
        y_ref[...] = jnp.zeros_like(y_ref)


def _expert_ffn(xs, block_e, n_active, wg, wu, wd, layer):
    d, dff = wg.shape[2], wg.shape[3]
    n_blocks = xs.shape[0] // (MOE_BLK * ROW_TILE)
    grid_spec = pltpu.PrefetchScalarGridSpec(
        num_scalar_prefetch=2,
        grid=(n_blocks,),
        in_specs=[pl.BlockSpec((MOE_BLK * ROW_TILE, LANES), lambda b, be, na: (jnp.minimum(b, na[0] - 1), 0)),
                  pl.BlockSpec((None, 1, d, dff), lambda b, be, na: (layer, be[b], 0, 0)),
                  pl.BlockSpec((None, 1, d, dff), lambda b, be, na: (layer, be[b], 0, 0)),
                  pl.BlockSpec((None, 1, dff, d), lambda b, be, na: (layer, be[b], 0, 0))],
        out_specs=pl.BlockSpec((MOE_BLK * ROW_TILE, LANES), lambda b, be, na: (b, 0)),
    )
    return pl.pallas_call(
        _ffn_kernel,
        grid_spec=grid_spec,
        out_shape=jax.ShapeDtypeStruct(xs.shape, u32),
        compiler_params=_cparams(("arbitrary",)),
    )(block_e, n_active, xs, wg, wu, wd)


def _combine_kernel(first_hbm, pairs_hbm, x1_ref, tw_ref, g_ref, b_ref, y_hbm, h32_ref, h16_ref,
                    ism, buf_a, buf_b, sem_i, sem_a, sem_b, *, tm, n_steps, alpha):
    j = pl.program_id(0)
    s = j % 2

    def pair_copy(step, sl):
        return _row_copy(pairs_hbm.at[step], ism.at[pl.ds(sl, 1)], sem_i.at[sl])

    def issue(buf, sem, sl, off):
        for r in range(tm):
            for k in range(2):
                d = ism[sl, off + k * tm + r]
                _row_copy(_token_rows(y_hbm, d), _token_rows(buf.at[k], r), sem).start(priority=k)

    def wait_buf(buf, sem):
        for k in range(2):
            _row_copy(_token_rows(y_hbm, 0, tm), buf.at[k], sem).wait()

    def finish(buf, t0):
        rs = slice(t0, t0 + tm)
        w0, w1 = tw_ref[rs, 0:1], tw_ref[rs, 1:2]
        rows0 = _load_token_rows(buf.at[0], tm)
        rows1 = _load_token_rows(buf.at[1], tm)
        half = ROW_TILE * LANES
        res = [None] * (2 * ROW_TILE)
        for c in range(ROW_TILE):
            lo0, hi0 = _unpack_pair(rows0[c])
            lo1, hi1 = _unpack_pair(rows1[c])
            res[c] = alpha * x1_ref[rs, c * LANES:(c + 1) * LANES] + (w0 * lo0 + w1 * lo1)
            res[ROW_TILE + c] = (alpha * x1_ref[rs, half + c * LANES:half + (c + 1) * LANES]
                                 + (w0 * hi0 + w1 * hi1))
        inv_d = 1.0 / (2 * half)
        mu = jnp.sum(functools.reduce(jnp.add, res), axis=-1, keepdims=True) * inv_d
        res = [r - mu for r in res]
        var = jnp.sum(functools.reduce(jnp.add, [r * r for r in res]), axis=-1, keepdims=True) * inv_d
        rstd = lax.rsqrt(var + LN_EPS)
        for c, r in enumerate(res):
            cs = slice(c * LANES, (c + 1) * LANES)
            o = r * rstd * g_ref[:, cs] + b_ref[:, cs]
            h32_ref[rs, cs] = o
            h16_ref[rs, cs] = o.astype(bf16)

    @pl.when(j == 0)
    def _():
        first = _row_copy(first_hbm.at[0], ism.at[pl.ds(0, 1), pl.ds(0, 2 * tm)], sem_i.at[0])
        first.start()
        first.wait()

        def body(r, carry):
            for k in range(2):
                d = ism[0, k * tm + r]
                _row_copy(_token_rows(y_hbm, d), _token_rows(buf_a.at[k], r), sem_a).start(priority=k)
            return carry

        lax.fori_loop(0, tm, body, 0, unroll=8)
        pair_copy(0, 0).start()

    pair_copy(j, s).wait()

    @pl.when(j + 1 < n_steps)
    def _():
        pair_copy(j + 1, 1 - s).start()

    wait_buf(buf_a, sem_a)
    issue(buf_b, sem_b, s, 0)
    finish(buf_a, 0)
    wait_buf(buf_b, sem_b)
    issue(buf_a, sem_a, s, 2 * tm)
    finish(buf_b, tm)

    @pl.when(j == n_steps - 1)
    def _():
        wait_buf(buf_a, sem_a)


def _combine(dest_tiles, x1, tw_t, ln_g, ln_b, y, alpha):
    m, d = x1.shape
    n_tiles = dest_tiles.shape[0]
    tm = dest_tiles.shape[2] // 2
    assert n_tiles % 2 == 0
    n_steps = n_tiles // 2
    nxt = jnp.minimum(2 * jnp.arange(n_steps) + 2, n_tiles - 1)
    pairs = jnp.concatenate([dest_tiles[1::2], dest_tiles[nxt]], axis=2)
    row = lambda i: (i, 0)
    fix = lambda i: (0, 0)
    half_buf = pltpu.VMEM((2, tm * ROW_TILE, LANES), u32)
    return pl.pallas_call(
        functools.partial(_combine_kernel, tm=tm, n_steps=n_steps, alpha=alpha),
        grid=(n_steps,),
        in_specs=[pl.BlockSpec(memory_space=pl.ANY), pl.BlockSpec(memory_space=pl.ANY),
                  pl.BlockSpec((2 * tm, d), row),
                  pl.BlockSpec((2 * tm, 2), row),
                  pl.BlockSpec((1, d), fix), pl.BlockSpec((1, d), fix),
                  pl.BlockSpec(memory_space=pl.ANY)],
        out_specs=[pl.BlockSpec((2 * tm, d), row), pl.BlockSpec((2 * tm, d), row)],
        out_shape=[jax.ShapeDtypeStruct((m, d), f32), jax.ShapeDtypeStruct((m, d), bf16)],
        scratch_shapes=[pltpu.SMEM((2, 4 * tm), i32), half_buf, half_buf,
                        pltpu.SemaphoreType.DMA((2,)), pltpu.SemaphoreType.DMA(()), pltpu.SemaphoreType.DMA(())],
        compiler_params=_cparams(("arbitrary",)),
    )(dest_tiles, pairs, x1, tw_t, ln_g, ln_b, y)


def _moe_plan(top_e, n_tok):
    rank, counts = _expert_ranks(top_e.reshape(1, 2 * n_tok))
    counts = counts.reshape(N_EXPERTS).astype(i32)
    padded = (counts + MOE_BLK - 1) // MOE_BLK * MOE_BLK
    pend = jnp.cumsum(padded)
    pstart = pend - padded
    e_flat = top_e.reshape(2 * n_tok)
    start_of = jnp.sum(jnp.where(e_flat[:, None] == jnp.arange(N_EXPERTS, dtype=i32)[None, :],
                                 pstart[None, :], 0), axis=1)
    dest = (start_of + rank.reshape(2 * n_tok)).reshape(2, n_tok)
    n_blocks = -(-(2 * n_tok) // MOE_BLK) + N_EXPERTS
    blk_row = jnp.arange(n_blocks, dtype=i32) * MOE_BLK
    block_e = jnp.minimum(jnp.sum((blk_row[:, None] >= pend[None, :]).astype(i32), axis=1), N_EXPERTS - 1)
    n_active = (pend[-1] // MOE_BLK).reshape(1).astype(i32)
    tm = _pick(n_tok, (TOK_TILE, 128))
    dest_tiles = dest.reshape(2, n_tok // tm, tm).transpose(1, 0, 2).reshape(n_tok // tm, 1, 2 * tm)
    n_slots = n_blocks * MOE_BLK
    pads = jnp.concatenate([pstart + counts, pend[-1:], padded - counts, n_slots - pend[-1:]]).astype(i32)
    return dest_tiles, block_e.astype(i32), n_active, pads, n_slots


def kernel(x, meta_tokens, w_in, w_gate_lr, b_gate_lr, sb_out_norm, gla_out_norm, w_out, ln1_g, ln1_b,
           w_router, b_router, w_ff_gate, w_ff_up, w_ff_down, ln2_g, ln2_b):
    bsz, seq, d = x.shape
    depth = w_in.shape[0]
    alpha = float((2 * depth) ** 0.25)
    length = N_META + seq
    lp = -(-length // SEQ_ALIGN) * SEQ_ALIGN
    pad = lp - length
    assert (pad - (-length) % GLA_CHUNK) % GLA_CHUNK == 0
    n_tok = bsz * lp

    meta = jnp.broadcast_to(meta_tokens[None].astype(x.dtype), (bsz, N_META, d))
    h = jnp.concatenate([jnp.zeros((bsz, pad, d), x.dtype), meta, x], axis=1).reshape(n_tok, d)
    h16 = h.astype(bf16)

    sb_w = SB_HEADS * SB_HEAD_DIM
    gla_kw = GLA_HEADS * GLA_DK
    gla_vw = GLA_HEADS * GLA_DV
    o_lr = 3 * sb_w + 2 * gla_kw + gla_vw
    o_go = o_lr + GLA_GATE_RANK
    w_main = jnp.concatenate([w_in[:, :, :o_lr], w_in[:, :, o_go:]], axis=2).astype(bf16)
    w_lr = jnp.pad(w_in[:, :, o_lr:o_go], ((0, 0), (0, 0), (0, LANES - GLA_GATE_RANK))).astype(bf16)
    w_g = jnp.pad(w_gate_lr, ((0, 0), (0, LANES - GLA_GATE_RANK), (0, 0))).astype(bf16)
    w_out16 = w_out.astype(bf16)
    wr_hi = w_router.astype(bf16)
    wr_lo = (w_router - wr_hi.astype(f32)).astype(bf16)
    wr = jnp.concatenate([jnp.pad(jnp.concatenate([wr_hi, wr_lo], axis=1), ((0, 0), (0, LANES - 2 * N_EXPERTS))),
                          jnp.pad(wr_hi, ((0, 0), (0, LANES - N_EXPERTS)))], axis=0)
    br = b_router.astype(f32).reshape(N_EXPERTS, 1)
    wg16, wu16, wd16 = w_ff_gate.astype(bf16), w_ff_up.astype(bf16), w_ff_down.astype(bf16)

    for l in range(depth):
        proj = _matmul(h16, w_main, l, bf16)
        log_a = _gla_gate(h16, w_lr[l], w_g[l], b_gate_lr[l].reshape(1, gla_kw))
        proj3 = proj.reshape(bsz, lp, proj.shape[1])
        y_sb = _sb_attention(proj3, sb_out_norm[l].reshape(1, sb_w), pad)
        y_gla = _gla(proj3, log_a.reshape(bsz, lp, gla_kw), gla_out_norm[l].reshape(1, gla_vw), pad)
        x1, x1p, top_e, top_w = _oproj_ln_route(
            y_sb.reshape(n_tok, sb_w), y_gla.reshape(n_tok, gla_vw), w_out16, l,
            h,ln1_g[l].reshape(1, d), ln1_b[l].reshape(1, d), wr, br, alpha)
        dest_tiles, block_e, n_active, pads, n_rows = _moe_plan(top_e, n_tok)
        xs = _dispatch(x1p, dest_tiles, pads, n_rows)
        y = _expert_ffn(xs, block_e, n_active, wg16, wu16, wd16, l)
        h, h16 = _combine(dest_tiles, x1, top_w.T, ln2_g[l].reshape(1, d), ln2_b[l].reshape(1, d), y, alpha)

    return h.reshape(bsz, lp, d)[:, pad + N_META:, :]
```

```python
import functools
import math

import jax
import jax.numpy as jnp
from jax import lax
from jax.experimental import pallas as pl
from jax.experimental.pallas import tpu as pltpu

f32 = jnp.float32
bf16 = jnp.bfloat16
u32 = jnp.uint32
i32 = jnp.int32

N_META = 16
SB_HEADS = 8
SB_HEAD_DIM = 128
GLA_HEADS = 4
GLA_DK = 128
GLA_DV = 256
GLA_GATE_RANK = 16
GLA_GATE_TAU = 16.0
GLA_CHUNK = 64
N_EXPERTS = 16
N_GROUPS = 4
EXPERTS_PER_GROUP = 4
LN_EPS = 1e-5
RMS_EPS = 1e-6

LANES = 128
VMEM_LIMIT = 56 * 1024 * 1024

SEQ_ALIGN = 256
SB_TQ = 128
SB_WINDOW = 3
SB_HEADS_PER_STEP = 8
GLA_ROWS = 256
OPROJ_ROWS = 128
MOE_BLK = 512
TOK_TILE = 256
RANK_CHUNK = 512
SB_DEAD_LOGW = 104.0


def _pick(n, cands):
    for c in cands:
        if n % c == 0:
            return c
    raise ValueError(f"no tile in {cands} divides {n}")


def _cparams(sem, vmem=VMEM_LIMIT):
    return pltpu.CompilerParams(dimension_semantics=sem, vmem_limit_bytes=vmem)


def _dot(a, b):
    return jnp.dot(a, b, preferred_element_type=f32)


def _dot_nt(a, b):
    return lax.dot_general(a, b, (((1,), (1,)), ((), ())), preferred_element_type=f32)


def _dot_tn(a, b):
    return lax.dot_general(a, b, (((0,), (0,)), ((), ())), preferred_element_type=f32)


def _pack_pair(lo, hi):
    lo_b = lax.bitcast_convert_type(lo.astype(bf16).astype(f32), u32)
    hi_b = lax.bitcast_convert_type(hi.astype(bf16).astype(f32), u32)
    return (hi_b & jnp.uint32(0xFFFF0000)) | (lo_b >> 16)


def _unpack_pair(w):
    lo = lax.bitcast_convert_type(w << 16, f32)
    hi = lax.bitcast_convert_type(w & jnp.uint32(0xFFFF0000), f32)
    return lo, hi


ROW_TILE = 8


def _store_token_rows(ref, words, t0=0):
    n = words.shape[0]
    for s in range(ROW_TILE):
        ref[pl.ds(t0 * ROW_TILE + s, n, stride=ROW_TILE), :] = words[:, s * LANES:(s + 1) * LANES]


def _load_token_rows(ref, n):
    return [ref[pl.ds(s, n, stride=ROW_TILE), :] for s in range(ROW_TILE)]


def _mm_kernel(x_ref, w_ref, o_ref):
    o_ref[...] = _dot(x_ref[...], w_ref[...]).astype(o_ref.dtype)


def _matmul(x, w, layer, out_dtype):
    m, k = x.shape
    n = w.shape[2]
    tm = _pick(m, (2048, 1024, 512, 256, 128))
    tn = _pick(n, (1024, 512, 256, 128))
    return pl.pallas_call(
        _mm_kernel,
        grid=(m // tm, n // tn),
        in_specs=[pl.BlockSpec((tm, k), lambda i, j: (i, 0)),
                  pl.BlockSpec((None, k, tn), lambda i, j: (layer, 0, j))],
        out_specs=pl.BlockSpec((tm, tn), lambda i, j: (i, j)),
        out_shape=jax.ShapeDtypeStruct((m, n), out_dtype),
        compiler_params=_cparams(("parallel", "parallel")),
    )(x, w)


def _gate_kernel(x_ref, wlr_ref, wg_ref, b_ref, o_ref):
    g_lr = _dot(x_ref[...], wlr_ref[...]).astype(bf16)
    y = _dot(g_lr, wg_ref[...]) + b_ref[...]
    log_sig = jnp.minimum(y, 0.0) - jnp.log(1.0 + jnp.exp(-jnp.abs(y)))
    o_ref[...] = log_sig * (1.0 / GLA_GATE_TAU)


def _gla_gate(x, w_lr, w_g, b_g):
    m, k = x.shape
    n = w_g.shape[1]
    tm = _pick(m, (512, 256, 128))
    return pl.pallas_call(
        _gate_kernel,
        grid=(m // tm,),
        in_specs=[pl.BlockSpec((tm, k), lambda i: (i, 0)),
                  pl.BlockSpec((k, LANES), lambda i: (0, 0)),
                  pl.BlockSpec((LANES, n), lambda i: (0, 0)),
                  pl.BlockSpec((1, n), lambda i: (0, 0))],
        out_specs=pl.BlockSpec((tm, n), lambda i: (i, 0)),
        out_shape=jax.ShapeDtypeStruct((m, n), f32),
        compiler_params=_cparams(("parallel",)),
    )(x, w_lr, w_g, b_g)


def _sb_blocks(q_ref, k_ref, v_ref, tri, heads, k0, n_blocks, valid, carries):
    tq, d = SB_TQ, SB_HEAD_DIM
    scale = d ** -0.5
    span = pl.ds(k0, n_blocks * tq)
    cols = [slice(h * d, (h + 1) * d) for h in range(heads)]
    z = [_dot_nt(q_ref[0, :, cs], k_ref[0, span, cs]) * scale for cs in cols]
    log_beta, sums = [], []
    for h in range(heads):
        for j in range(n_blocks):
            zj = z[h][:, j * tq:(j + 1) * tq]
            sp = jnp.maximum(zj, 0.0) + jnp.log(1.0 + jnp.exp(-jnp.abs(zj)))
            log_beta.append(zj - sp)
            sums.append(_dot(jnp.where(valid[j], sp, 0.0).astype(bf16), tri))
    outs, new_carries = [], []
    for h in range(heads):
        carry = carries[h]
        w = [None] * n_blocks
        for j in reversed(range(n_blocks)):
            sm = sums[h * n_blocks + j]
            between = sm[:, :tq] + carry
            w[j] = jnp.where(valid[j], jnp.exp(log_beta[h * n_blocks + j] - between), 0.0).astype(bf16)
            carry = carry + sm[:, tq:]
        outs.append(_dot(w[0] if n_blocks == 1 else jnp.concatenate(w, axis=1), v_ref[0, span, cols[h]]))
        new_carries.append(carry)
    return outs, new_carries


def _sb_kernel(q_ref, k_ref, v_ref, tri_ref, g_ref, o_ref, *, pad, heads):
    tq, d = SB_TQ, SB_HEAD_DIM
    i = pl.program_id(2)
    t0 = i * tq
    wb = jnp.maximum(i - (SB_WINDOW - 1), 0)
    ws = pl.multiple_of(wb * tq, tq)
    key_minus_query = lax.broadcasted_iota(i32, (tq, tq), 1) - lax.broadcasted_iota(i32, (tq, tq), 0)
    col = lax.broadcasted_iota(i32, (tq, tq), 1)
    tri = tri_ref[...]

    def valid_for(k0):
        return jnp.logical_and(key_minus_query < t0 - k0, col >= pad - k0)

    def lowest(carries):
        m = jnp.min(carries[0])
        for c in carries[1:]:
            m = jnp.minimum(m, jnp.min(c))
        return m

    zero = [jnp.zeros((tq, tq), f32)] * heads
    accs, carries = _sb_blocks(q_ref, k_ref, v_ref, tri, heads, ws, SB_WINDOW,
                               [valid_for(ws + j * tq) for j in range(SB_WINDOW)], zero)

    def cond(c):
        jb, cmin, _, _ = c
        return jnp.logical_and(jb >= 0, cmin < SB_DEAD_LOGW)

    def body(c):
        jb, _, accs, carries = c
        k0 = pl.multiple_of(jb * tq, tq)
        outs, carries = _sb_blocks(q_ref, k_ref, v_ref, tri, heads, k0, 1, [valid_for(k0)], list(carries))
        return jb - 1, lowest(carries), tuple(a + o for a, o in zip(accs, outs)), tuple(carries)

    _, _, accs, _ = lax.while_loop(cond, body, (wb - 1, lowest(carries), tuple(accs), tuple(carries)))
    for h in range(heads):
        cs = slice(h * d, (h + 1) * d)
        ms = jnp.mean(accs[h] * accs[h], axis=1, keepdims=True)
        o_ref[0, :, cs] = (accs[h] * lax.rsqrt(ms + RMS_EPS) * g_ref[:, cs]).astype(o_ref.dtype)


def _sb_attention(proj, norm_g, pad):
    b, lp, _ = proj.shape
    tq, heads = SB_TQ, SB_HEADS_PER_STEP
    wid = heads * SB_HEAD_DIM
    groups = SB_HEADS // heads
    later = jnp.arange(tq)[:, None] > jnp.arange(tq)[None, :]
    tri = jnp.concatenate([later, jnp.ones((tq, tq), bool)], axis=1).astype(bf16)
    return pl.pallas_call(
        functools.partial(_sb_kernel, pad=pad, heads=heads),
        grid=(b, groups, lp // tq),
        in_specs=[pl.BlockSpec((1, tq, wid), lambda bi, h, i: (bi, i, h)),
                  pl.BlockSpec((1, lp, wid), lambda bi, h, i: (bi, 0, groups + h)),
                  pl.BlockSpec((1, lp, wid), lambda bi, h, i: (bi, 0, 2 * groups + h)),
                  pl.BlockSpec((tq, 2 * tq), lambda bi, h, i: (0, 0)),
                  pl.BlockSpec((1, wid), lambda bi, h, i: (0, h))],
        out_specs=pl.BlockSpec((1, tq, wid), lambda bi, h, i: (bi, i, h)),
        out_shape=jax.ShapeDtypeStruct((b, lp, SB_HEADS * SB_HEAD_DIM), bf16),
        compiler_params=_cparams(("parallel", "parallel", "parallel")),
    )(proj, proj, proj, tri, norm_g)


def _gla_kernel(q_ref, k_ref, v_ref, la_ref, go_ref, gn_ref, o_ref, st_ref, *, pad):
    c, rows, dk, dv = GLA_CHUNK, GLA_ROWS, GLA_DK, GLA_DV
    s = pl.program_id(1)

    @pl.when(s == 0)
    def _():
        st_ref[...] = jnp.zeros_like(st_ref)

    real = (s * rows + lax.broadcasted_iota(i32, (rows, 1), 0)) >= pad
    r_i = lax.broadcasted_iota(i32, (rows, rows), 0)
    c_i = lax.broadcasted_iota(i32, (rows, rows), 1)
    shift = c.bit_length() - 1
    causal = jnp.logical_and(c_i <= r_i, (r_i >> shift) == (c_i >> shift))
    causal_b = causal.astype(bf16)
    scale = dk ** -0.5

    heads = range(GLA_HEADS)
    ks = [slice(h * dk, (h + 1) * dk) for h in heads]
    vs = [slice(h * dv, (h + 1) * dv) for h in heads]
    bcum = []
    for h in heads:
        g = jnp.where(real, la_ref[0, :, ks[h]], 0.0)
        g_hi = g.astype(bf16)
        g_lo = (g - g_hi.astype(f32)).astype(bf16)
        bcum.append(_dot(causal_b, g_hi) + _dot(causal_b, g_lo))
    kf = [jnp.where(real, k_ref[0, :, ks[h]].astype(f32), 0.0) for h in heads]
    q_dec = [(q_ref[0, :, ks[h]].astype(f32) * scale * jnp.exp(bcum[h])).astype(bf16) for h in heads]
    scores = [jnp.where(causal, _dot_nt(q_dec[h], (kf[h] * jnp.exp(-bcum[h])).astype(bf16)), 0.0) for h in heads]
    o_intra = [_dot(scores[h].astype(bf16), v_ref[0, :, vs[h]]) for h in heads]
    st = [st_ref[h] for h in heads]
    o_inter = [[] for _ in heads]
    for n in range(rows // c):
        rs = slice(n * c, (n + 1) * c)
        for h in heads:
            b_last = bcum[h][(n + 1) * c - 1:(n + 1) * c, :]
            k_dec = (kf[h][rs] * jnp.exp(b_last - bcum[h][rs])).astype(bf16)
            o_inter[h].append(_dot_nt(q_dec[h][rs], st[h].astype(bf16)))
            st[h] = st[h] * jnp.exp(b_last) + _dot_tn(v_ref[0, rs, vs[h]], k_dec)
    for h in heads:
        st_ref[h] = st[h]
        o = o_intra[h] + jnp.concatenate(o_inter[h], axis=0)
        ms = jnp.mean(o * o, axis=1, keepdims=True)
        gate = go_ref[0, :, vs[h]].astype(f32)
        y = o * lax.rsqrt(ms + RMS_EPS) * gn_ref[:, vs[h]] * (gate * jax.nn.sigmoid(gate))
        o_ref[0, :, vs[h]] = y.astype(o_ref.dtype)


def _gla(proj, log_a, norm_g, pad):
    b, lp, _ = proj.shape
    rows = GLA_ROWS
    kw, vw = GLA_HEADS * GLA_DK, GLA_HEADS * GLA_DV
    return pl.pallas_call(
        functools.partial(_gla_kernel, pad=pad),
        grid=(b, lp // rows),
        in_specs=[pl.BlockSpec((1, rows, kw), lambda bi, s: (bi, s, 6)),
                  pl.BlockSpec((1, rows, kw), lambda bi, s: (bi, s, 7)),
                  pl.BlockSpec((1, rows, vw), lambda bi, s: (bi, s, 4)),
                  pl.BlockSpec((1, rows, kw), lambda bi, s: (bi, s, 0)),
                  pl.BlockSpec((1, rows, vw), lambda bi, s: (bi, s, 5)),
                  pl.BlockSpec((1, vw), lambda bi, s: (0, 0))],
        out_specs=pl.BlockSpec((1, rows, vw), lambda bi, s: (bi, s, 0)),
        out_shape=jax.ShapeDtypeStruct((b, lp, vw), bf16),
        scratch_shapes=[pltpu.VMEM((GLA_HEADS, GLA_DV, GLA_DK), f32)],
        compiler_params=_cparams(("parallel", "arbitrary")),
    )(proj, proj, proj, log_a, proj, norm_g)


def _layer_norm_rows(r, g, b):
    mu = jnp.mean(r, axis=-1, keepdims=True)
    d = r - mu
    var = jnp.mean(d * d, axis=-1, keepdims=True)
    return d * lax.rsqrt(var + LN_EPS) * g + b


def _route(logits):
    m = jnp.max(logits, axis=0, keepdims=True)
    e = jnp.exp(logits - m)
    aff = e / jnp.sum(e, axis=0, keepdims=True)
    rows = [aff[j:j + 1, :] for j in range(N_EXPERTS)]
    scores = []
    for g in range(N_GROUPS):
        a, b, c, d = rows[4 * g:4 * g + 4]
        hi1, lo1 = jnp.maximum(a, b), jnp.minimum(a, b)
        hi2, lo2 = jnp.maximum(c, d), jnp.minimum(c, d)
        scores.append(jnp.maximum(hi1, hi2) + jnp.maximum(jnp.minimum(hi1, hi2), jnp.maximum(lo1, lo2)))
    best = scores[0]
    g_sel = jnp.zeros(best.shape, i32)
    for g in range(1, N_GROUPS):
        better = scores[g] > best
        g_sel = jnp.where(better, g, g_sel)
        best = jnp.where(better, scores[g], best)
    sel = list(rows[0:4])
    for g in range(1, N_GROUPS):
        here = g_sel == g
        sel = [jnp.where(here, rows[4 * g + j], sel[j]) for j in range(4)]
    v1, i1 = sel[0], jnp.zeros(best.shape, i32)
    for j in range(1, 4):
        gt = sel[j] > v1
        i1 = jnp.where(gt, j, i1)
        v1 = jnp.where(gt, sel[j], v1)
    v2, i2 = jnp.full(best.shape, -1.0, f32), jnp.zeros(best.shape, i32)
    for j in range(4):
        cand = jnp.where(i1 == j, -1.0, sel[j])
        gt = cand > v2
        i2 = jnp.where(gt, j, i2)
        v2 = jnp.where(gt, cand, v2)
    s = v1 + v2
    top_e = jnp.concatenate([4 * g_sel + i1, 4 * g_sel + i2], axis=0)
    top_w = jnp.concatenate([v1 / s, v2 / s], axis=0)
    return top_e, top_w


def _oproj_kernel(ysb_ref, ygla_ref, w1_ref, w2_ref, h_ref, g_ref, b_ref, wr_ref, br_ref,
                  x1p_ref, te_ref, tw_ref, *, alpha):
    tm, d = h_ref.shape
    half = d // 2
    n = tm // OPROJ_ROWS
    groups = [slice(p * OPROJ_ROWS, (p + 1) * OPROJ_ROWS) for p in range(n)]
    mixes = [_dot(ysb_ref[rs, :], w1_ref[...]) + _dot(ygla_ref[rs, :], w2_ref[...]) for rs in groups]
    wr = wr_ref[...]
    for p, rs in enumerate(groups):
        x1 = _layer_norm_rows(alpha * h_ref[rs, :].astype(f32) + mixes[p], g_ref[...], b_ref[...])
        _store_token_rows(x1p_ref, _pack_pair(x1[:, :half], x1[:, half:]), p * OPROJ_ROWS)
        xh = x1.astype(bf16)
        xl = (x1 - xh.astype(f32)).astype(bf16)
        l_t = (_dot(xh, wr[:d]) + _dot(xl, wr[d:])).T
        logits = l_t[:N_EXPERTS] + l_t[N_EXPERTS:2 * N_EXPERTS] + br_ref[...]
        top_e, top_w = _route(logits)
        te_ref[:, rs] = top_e
        tw_ref[:, rs] = top_w


def _oproj_ln_route(y_sb, y_gla, w_out, layer, h, ln_g, ln_b, wr, br, alpha):
    m, d = h.shape
    kh = y_sb.shape[1]
    tm = _pick(m, (512, 256, 128))
    row = lambda i: (i, 0)
    fix = lambda i: (0, 0)
    col = lambda i: (0, i)
    return pl.pallas_call(
        functools.partial(_oproj_kernel, alpha=alpha),
        grid=(m // tm,),
        in_specs=[pl.BlockSpec((tm, kh), row), pl.BlockSpec((tm, kh), row),
                  pl.BlockSpec((None, kh, d), lambda i: (layer, 0, 0)),
                  pl.BlockSpec((None, kh, d), lambda i: (layer, 1, 0)),
                  pl.BlockSpec((tm, d), row),
                  pl.BlockSpec((1, d), fix), pl.BlockSpec((1, d), fix),
                  pl.BlockSpec((2 * d, LANES), fix), pl.BlockSpec((N_EXPERTS, 1), fix)],
        out_specs=[pl.BlockSpec((tm * ROW_TILE, LANES), row),
                   pl.BlockSpec((2, tm), col), pl.BlockSpec((2, tm), col)],
        out_shape=[jax.ShapeDtypeStruct((m * ROW_TILE, LANES), u32),
                   jax.ShapeDtypeStruct((2, m), i32), jax.ShapeDtypeStruct((2, m), f32)],
        compiler_params=_cparams(("parallel",)),
    )(y_sb, y_gla, w_out, w_out, h, ln_g, ln_b, wr, br)


def _rank_kernel(e_ref, rank_ref, cnt_ref, carry_ref, *, chunk):
    @pl.when(pl.program_id(0) == 0)
    def _():
        carry_ref[...] = jnp.zeros_like(carry_ref)

    e = e_ref[...]
    onehot = lax.broadcasted_iota(i32, (N_EXPERTS, chunk), 0) == e
    before = (lax.broadcasted_iota(i32, (chunk, chunk), 0)
              < lax.broadcasted_iota(i32, (chunk, chunk), 1)).astype(bf16)
    seen = _dot(onehot.astype(bf16), before) + carry_ref[...]
    rank_ref[...] = jnp.sum(jnp.where(onehot, seen, 0.0), axis=0, keepdims=True).astype(i32)
    total = carry_ref[...] + jnp.sum(onehot.astype(f32), axis=1, keepdims=True)
    carry_ref[...] = total
    cnt_ref[...] = total


def _expert_ranks(e_flat):
    n = e_flat.shape[1]
    chunk = _pick(n, (RANK_CHUNK, 256, 128))
    return pl.pallas_call(
        functools.partial(_rank_kernel, chunk=chunk),
        grid=(n // chunk,),
        in_specs=[pl.BlockSpec((1, chunk), lambda i: (0, i))],
        out_specs=[pl.BlockSpec((1, chunk), lambda i: (0, i)),
                   pl.BlockSpec((N_EXPERTS, 1), lambda i: (0, 0))],
        out_shape=[jax.ShapeDtypeStruct((1, n), i32), jax.ShapeDtypeStruct((N_EXPERTS, 1), f32)],
        scratch_shapes=[pltpu.VMEM((N_EXPERTS, 1), f32)],
        compiler_params=_cparams(("arbitrary",)),
    )(e_flat)


def _row_copy(src, dst, sem):
    return pltpu.make_async_copy(src, dst, sem)


def _token_rows(ref, t, n=1):
    start = t * ROW_TILE
    if not isinstance(start, int):
        start = pl.multiple_of(start, ROW_TILE)
    return ref.at[pl.ds(start, n * ROW_TILE)]


ZERO_CHUNK = 64


def _dispatch_kernel(pads_ref, dest_hbm, x_ref, xs_hbm, dsm, zbuf, sem_i, sem_d, sem_z, *, tm, n_tiles):
    i = pl.program_id(0)
    slot = i % 2

    def idx_copy(t, sl):
        return _row_copy(dest_hbm.at[t], dsm.at[pl.ds(sl, 1)], sem_i.at[sl])

    def zero_copy(first, n):
        return _row_copy(_token_rows(zbuf, 0, n), _token_rows(xs_hbm, first, n), sem_z)

    @pl.when(i == 0)
    def _():
        idx_copy(0, 0).start()
        zbuf[...] = jnp.zeros_like(zbuf)
        n_big, n_one = 0, 0
        n_runs = pads_ref.shape[0] // 2
        for e in range(n_runs):
            first, count = pads_ref[e], pads_ref[n_runs + e]
            big = count >> (ZERO_CHUNK.bit_length() - 1)
            rest = count - big * ZERO_CHUNK

            def chunk(c, carry, first=first):
                zero_copy(first + c * ZERO_CHUNK, ZERO_CHUNK).start()
                return carry

            def single(r, carry, first=first + big * ZERO_CHUNK):
                zero_copy(first + r, 1).start()
                return carry

            lax.fori_loop(0, big, chunk, 0)
            lax.fori_loop(0, rest, single, 0)
            n_big, n_one = n_big + big, n_one + rest
        lax.fori_loop(0, n_big, lambda c, carry: (zero_copy(0, ZERO_CHUNK).wait(), carry)[1], 0)
        lax.fori_loop(0, n_one, lambda c, carry: (zero_copy(0, 1).wait(), carry)[1], 0)

    idx_copy(i, slot).wait()

    @pl.when(i + 1 < n_tiles)
    def _():
        idx_copy(i + 1, 1 - slot).start()

    for r in range(tm):
        for k in range(2):
            d = dsm[slot, k * tm + r]
            _row_copy(_token_rows(x_ref, r), _token_rows(xs_hbm, d), sem_d).start(priority=k)
    for _ in range(2):
        _row_copy(x_ref, _token_rows(xs_hbm, 0, tm), sem_d).wait()


def _dispatch(x_rows, dest_tiles, pads, n_slots):
    n_tiles = dest_tiles.shape[0]
    tm = dest_tiles.shape[2] // 2
    grid_spec = pltpu.PrefetchScalarGridSpec(
        num_scalar_prefetch=1,
        grid=(n_tiles,),
        in_specs=[pl.BlockSpec(memory_space=pl.ANY),
                  pl.BlockSpec((tm * ROW_TILE, LANES), lambda i, pads: (i, 0))],
        out_specs=pl.BlockSpec(memory_space=pl.ANY),
        scratch_shapes=[pltpu.SMEM((2, 2 * tm), i32), pltpu.VMEM((ZERO_CHUNK * ROW_TILE, LANES), u32),
                        pltpu.SemaphoreType.DMA((2,)), pltpu.SemaphoreType.DMA(()), pltpu.SemaphoreType.DMA(())],
    )
    return pl.pallas_call(
        functools.partial(_dispatch_kernel, tm=tm, n_tiles=n_tiles),
        grid_spec=grid_spec,
        out_shape=jax.ShapeDtypeStruct((n_slots * ROW_TILE, LANES), u32),
        compiler_params=_cparams(("arbitrary",)),
    )(pads, dest_tiles, x_rows)


def _ffn_kernel(be_ref, na_ref, xs_ref, wg_ref, wu_ref, wd_ref, y_ref):
    del be_ref
    blk = pl.program_id(0)

    @pl.when(blk < na_ref[0])
    def _():
        parts = [_unpack_pair(w) for w in _load_token_rows(xs_ref, MOE_BLK)]
        lo = jnp.concatenate([p[0].astype(bf16) for p in parts], axis=1)
        hi = jnp.concatenate([p[1].astype(bf16) for p in parts], axis=1)
        half = lo.shape[1]
        gate = _dot(lo, wg_ref[0, :half, :]) + _dot(hi, wg_ref[0, half:, :])
        up = _dot(lo, wu_ref[0, :half, :]) + _dot(hi, wu_ref[0, half:, :])
        hid = (gate * jax.nn.sigmoid(gate) * up).astype(bf16)
        out = _dot(hid, wd_ref[0])
        _store_token_rows(y_ref, _pack_pair(out[:, :half], out[:, half:]))

    @pl.when(blk >= na_ref[0])
    def _():
        y_ref[...] = jnp.zeros_like(y_ref)


def _expert_ffn(xs, block_e, n_active, wg, wu, wd, layer):
    d, dff = wg.shape[2], wg.shape[3]
    n_blocks = xs.shape[0] // (MOE_BLK * ROW_TILE)
    grid_spec = pltpu.PrefetchScalarGridSpec(
        num_scalar_prefetch=2,
        grid=(n_blocks,),
        in_specs=[pl.BlockSpec((MOE_BLK * ROW_TILE, LANES), lambda b, be, na: (jnp.minimum(b, na[0] - 1), 0)),
                  pl.BlockSpec((None, 1, d, dff), lambda b, be, na: (layer, be[b], 0, 0)),
                  pl.BlockSpec((None, 1, d, dff), lambda b, be, na: (layer, be[b], 0, 0)),
                  pl.BlockSpec((None, 1, dff, d), lambda b, be, na: (layer, be[b], 0, 0))],
        out_specs=pl.BlockSpec((MOE_BLK * ROW_TILE, LANES), lambda b, be, na: (b, 0)),
    )
    return pl.pallas_call(
        _ffn_kernel,
        grid_spec=grid_spec,
        out_shape=jax.ShapeDtypeStruct(xs.shape, u32),
        compiler_params=_cparams(("arbitrary",)),
    )(block_e, n_active, xs, wg, wu, wd)


def _combine_kernel(first_hbm, pairs_hbm, x1p_ref, tw_ref, g_ref, b_ref, y_hbm, out_ref,
                    ism, buf_a, buf_b, sem_i, sem_a, sem_b, *, tm, n_steps, alpha):
    j = pl.program_id(0)
    s = j % 2

    def pair_copy(step, sl):
        return _row_copy(pairs_hbm.at[step], ism.at[pl.ds(sl, 1)], sem_i.at[sl])

    def issue(buf, sem, sl, off):
        for r in range(tm):
            for k in range(2):
                d = ism[sl, off + k * tm + r]
                _row_copy(_token_rows(y_hbm, d), _token_rows(buf.at[k], r), sem).start(priority=k)

    def wait_buf(buf, sem):
        for k in range(2):
            _row_copy(_token_rows(y_hbm, 0, tm), buf.at[k], sem).wait()

    def finish(buf, t0):
        rs = slice(t0, t0 + tm)
        w0, w1 = tw_ref[rs, 0:1], tw_ref[rs, 1:2]
        rows0 = _load_token_rows(buf.at[0], tm)
        rows1 = _load_token_rows(buf.at[1], tm)
        rows_x = _load_token_rows(_token_rows(x1p_ref, t0, tm), tm)
        res = [None] * (2 * ROW_TILE)
        for c in range(ROW_TILE):
            lo0, hi0 = _unpack_pair(rows0[c])
            lo1, hi1 = _unpack_pair(rows1[c])
            lox, hix = _unpack_pair(rows_x[c])
            res[c] = alpha * lox + (w0 * lo0 + w1 * lo1)
            res[ROW_TILE + c] = alpha * hix + (w0 * hi0 + w1 * hi1)
        inv_d = 1.0 / (2 * ROW_TILE * LANES)
        mu = jnp.sum(functools.reduce(jnp.add, res), axis=-1, keepdims=True) * inv_d
        res = [r - mu for r in res]
        var = jnp.sum(functools.reduce(jnp.add, [r * r for r in res]), axis=-1, keepdims=True) * inv_d
        rstd = lax.rsqrt(var + LN_EPS)
        for c, r in enumerate(res):
            cs = slice(c * LANES, (c + 1) * LANES)
            out_ref[rs, cs] = (r * rstd * g_ref[:, cs] + b_ref[:, cs]).astype(out_ref.dtype)

    @pl.when(j == 0)
    def _():
        first = _row_copy(first_hbm.at[0], ism.at[pl.ds(0, 1), pl.ds(0, 2 * tm)], sem_i.at[0])
        first.start()
        first.wait()

        def body(r, carry):
            for k in range(2):
                d = ism[0, k * tm + r]
                _row_copy(_token_rows(y_hbm, d), _token_rows(buf_a.at[k], r), sem_a).start(priority=k)
            return carry

        lax.fori_loop(0, tm, body, 0, unroll=8)
        pair_copy(0, 0).start()

    pair_copy(j, s).wait()

    @pl.when(j + 1 < n_steps)
    def _():
        pair_copy(j + 1, 1 - s).start()

    wait_buf(buf_a, sem_a)
    issue(buf_b, sem_b, s, 0)
    finish(buf_a, 0)
    wait_buf(buf_b, sem_b)
    issue(buf_a, sem_a, s, 2 * tm)
    finish(buf_b, tm)

    @pl.when(j == n_steps - 1)
    def _():
        wait_buf(buf_a, sem_a)


def _combine(dest_tiles, x1p, tw_t, ln_g, ln_b, y, alpha, out_dtype):
    m, d = tw_t.shape[0], ln_g.shape[1]
    n_tiles = dest_tiles.shape[0]
    tm = dest_tiles.shape[2] // 2
    assert n_tiles % 2 == 0
    n_steps = n_tiles // 2
    nxt = jnp.minimum(2 * jnp.arange(n_steps) + 2, n_tiles - 1)
    pairs = jnp.concatenate([dest_tiles[1::2], dest_tiles[nxt]], axis=2)
    row = lambda i: (i, 0)
    fix = lambda i: (0, 0)
    half_buf = pltpu.VMEM((2, tm * ROW_TILE, LANES), u32)
    return pl.pallas_call(
        functools.partial(_combine_kernel, tm=tm, n_steps=n_steps, alpha=alpha),
        grid=(n_steps,),
        in_specs=[pl.BlockSpec(memory_space=pl.ANY), pl.BlockSpec(memory_space=pl.ANY),
                  pl.BlockSpec((2 * tm * ROW_TILE, LANES), row),
                  pl.BlockSpec((2 * tm, 2), row),
                  pl.BlockSpec((1, d), fix), pl.BlockSpec((1, d), fix),
                  pl.BlockSpec(memory_space=pl.ANY)],
        out_specs=pl.BlockSpec((2 * tm, d), row),
        out_shape=jax.ShapeDtypeStruct((m, d), out_dtype),
        scratch_shapes=[pltpu.SMEM((2, 4 * tm), i32), half_buf, half_buf,
                        pltpu.SemaphoreType.DMA((2,)), pltpu.SemaphoreType.DMA(()), pltpu.SemaphoreType.DMA(())],
        compiler_params=_cparams(("arbitrary",)),
    )(dest_tiles, pairs, x1p, tw_t, ln_g, ln_b, y)


def _moe_plan(top_e, n_tok):
    rank, counts = _expert_ranks(top_e.reshape(1, 2 * n_tok))
    counts = counts.reshape(N_EXPERTS).astype(i32)
    padded = (counts + MOE_BLK - 1) // MOE_BLK * MOE_BLK
    pend = jnp.cumsum(padded)
    pstart = pend - padded
    e_flat = top_e.reshape(2 * n_tok)
    start_of = jnp.sum(jnp.where(e_flat[:, None] == jnp.arange(N_EXPERTS, dtype=i32)[None, :],
                                 pstart[None, :], 0), axis=1)
    dest = (start_of + rank.reshape(2 * n_tok)).reshape(2, n_tok)
    n_blocks = -(-(2 * n_tok) // MOE_BLK) + N_EXPERTS
    blk_row = jnp.arange(n_blocks, dtype=i32) * MOE_BLK
    block_e = jnp.minimum(jnp.sum((blk_row[:, None] >= pend[None, :]).astype(i32), axis=1), N_EXPERTS - 1)
    n_active = (pend[-1] // MOE_BLK).reshape(1).astype(i32)
    tm = _pick(n_tok, (TOK_TILE, 128))
    dest_tiles = dest.reshape(2, n_tok // tm, tm).transpose(1, 0, 2).reshape(n_tok // tm, 1, 2 * tm)
    n_slots = n_blocks * MOE_BLK
    pads = jnp.concatenate([pstart + counts, pend[-1:], padded - counts, n_slots - pend[-1:]]).astype(i32)
    return dest_tiles, block_e.astype(i32), n_active, pads, n_slots


def kernel(x, meta_tokens, w_in, w_gate_lr, b_gate_lr, sb_out_norm, gla_out_norm, w_out, ln1_g, ln1_b,
           w_router, b_router, w_ff_gate, w_ff_up, w_ff_down, ln2_g, ln2_b):
    bsz, seq, d = x.shape
    depth = w_in.shape[0]
    alpha = float((2 * depth) ** 0.25)
    length = N_META + seq
    lp = -(-length // SEQ_ALIGN) * SEQ_ALIGN
    pad = lp - length
    assert (pad - (-length) % GLA_CHUNK) % GLA_CHUNK == 0
    n_tok = bsz * lp

    meta = jnp.broadcast_to(meta_tokens[None].astype(bf16), (bsz, N_META, d))
    h16 = jnp.concatenate([jnp.zeros((bsz, pad, d), bf16), meta, x.astype(bf16)], axis=1).reshape(n_tok, d)

    sb_w = SB_HEADS * SB_HEAD_DIM
    gla_kw = GLA_HEADS * GLA_DK
    gla_vw = GLA_HEADS * GLA_DV
    o_lr = 3 * sb_w + 2 * gla_kw + gla_vw
    o_go = o_lr + GLA_GATE_RANK
    w_main = jnp.concatenate([w_in[:, :, :o_lr], w_in[:, :, o_go:]], axis=2).astype(bf16)
    w_lr = jnp.pad(w_in[:, :, o_lr:o_go], ((0, 0), (0, 0), (0, LANES - GLA_GATE_RANK))).astype(bf16)
    w_g = jnp.pad(w_gate_lr, ((0, 0), (0, LANES - GLA_GATE_RANK), (0, 0))).astype(bf16)
    w_out16 = w_out.astype(bf16)
    wr_hi = w_router.astype(bf16)
    wr_lo = (w_router - wr_hi.astype(f32)).astype(bf16)
    wr = jnp.concatenate([jnp.pad(jnp.concatenate([wr_hi, wr_lo], axis=1), ((0, 0), (0, LANES - 2 * N_EXPERTS))),
                          jnp.pad(wr_hi, ((0, 0), (0, LANES - N_EXPERTS)))], axis=0)
    br = b_router.astype(f32).reshape(N_EXPERTS, 1)
    wg16, wu16, wd16 = w_ff_gate.astype(bf16), w_ff_up.astype(bf16), w_ff_down.astype(bf16)

    for l in range(depth):
        proj = _matmul(h16, w_main, l, bf16)
        log_a = _gla_gate(h16, w_lr[l], w_g[l], b_gate_lr[l].reshape(1, gla_kw))
        proj3 = proj.reshape(bsz, lp, proj.shape[1])
        y_sb = _sb_attention(proj3, sb_out_norm[l].reshape(1, sb_w), pad)
        y_gla = _gla(proj3, log_a.reshape(bsz, lp, gla_kw), gla_out_norm[l].reshape(1, gla_vw), pad)
        x1p, top_e, top_w = _oproj_ln_route(
            y_sb.reshape(n_tok, sb_w), y_gla.reshape(n_tok, gla_vw), w_out16, l,
            h16, ln1_g[l].reshape(1, d), ln1_b[l].reshape(1, d), wr, br, alpha)
        dest_tiles, block_e, n_active, pads, n_rows = _moe_plan(top_e, n_tok)
        xs = _dispatch(x1p, dest_tiles, pads, n_rows)
        y = _expert_ffn(xs, block_e, n_active, wg16, wu16, wd16, l)
        h16 = _combine(dest_tiles, x1p, top_w.T, ln2_g[l].reshape(1, d), ln2_b[l].reshape(1, d), y, alpha,
                       bf16 if l + 1 < depth else x.dtype)

    return h16.reshape(bsz, lp, d)[:, pad + N_META:, :]
```

```python
import functools
import math

import jax
import jax.numpy as jnp
from jax import lax
from jax.experimental import pallas as pl
from jax.experimental.pallas import tpu as pltpu

f32 = jnp.float32
bf16 = jnp.bfloat16
i32 = jnp.int32

N_META = 16
SB_HEADS = 8
SB_HEAD_DIM = 128
GLA_HEADS = 4
GLA_DK = 128
GLA_DV = 256
GLA_GATE_RANK = 16
GLA_GATE_TAU = 16.0
GLA_CHUNK = 64
N_EXPERTS = 16
N_GROUPS = 4
EXPERTS_PER_GROUP = 4
LN_EPS = 1e-5
RMS_EPS = 1e-6

LANES = 128
VMEM_LIMIT = 56 * 1024 * 1024

SEQ_ALIGN = 256
SB_TQ = 128
SB_WINDOW = 3
SB_HEADS_PER_STEP = 8
GLA_ROWS = 256
OPROJ_ROWS = 128
MOE_BLK = 512
TOK_TILE = 256
RANK_CHUNK = 512
SB_DEAD_LOGW = 104.0


def _pick(n, cands):
    for c in cands:
        if n % c == 0:
            return c
    raise ValueError(f"no tile in {cands} divides {n}")


def _cparams(sem, vmem=VMEM_LIMIT):
    return pltpu.CompilerParams(dimension_semantics=sem, vmem_limit_bytes=vmem)


def _dot(a, b):
    return jnp.dot(a, b, preferred_element_type=f32)


def _dot_nt(a, b):
    return lax.dot_general(a, b, (((1,), (1,)), ((), ())), preferred_element_type=f32)


def _dot_tn(a, b):
    return lax.dot_general(a, b, (((0,), (0,)), ((), ())), preferred_element_type=f32)


ROW_TILE = 16


def _store_token_rows(ref, words, t0=0):
    n = words.shape[0]
    for s in range(ROW_TILE):
        ref[pl.ds(t0 * ROW_TILE + s, n, stride=ROW_TILE), :] = words[:, s * LANES:(s + 1) * LANES]


def _load_token_rows(ref, n):
    return [ref[pl.ds(s, n, stride=ROW_TILE), :] for s in range(ROW_TILE)]


def _embed_kernel(x_ref, meta_ref, o_ref, *, pad):
    j = pl.program_id(1)

    @pl.when(j == 0)
    def _():
        o_ref[0, :pad, :] = jnp.zeros((pad, o_ref.shape[2]), o_ref.dtype)
        o_ref[0, pad:, :] = meta_ref[...].astype(o_ref.dtype)

    @pl.when(j > 0)
    def _():
        o_ref[0] = x_ref[0].astype(o_ref.dtype)


def _embed(x, meta_tokens, pad):
    bsz, seq, d = x.shape
    tile = pad + N_META
    assert seq % tile == 0 and pad % 16 == 0
    return pl.pallas_call(
        functools.partial(_embed_kernel, pad=pad),
        grid=(bsz, seq // tile + 1),
        in_specs=[pl.BlockSpec((1, tile, d), lambda b, j: (b, jnp.maximum(j - 1, 0), 0)),
                  pl.BlockSpec((N_META, d), lambda b, j: (0, 0))],
        out_specs=pl.BlockSpec((1, tile, d), lambda b, j: (b, j, 0)),
        out_shape=jax.ShapeDtypeStruct((bsz, tile + seq, d), bf16),
        compiler_params=_cparams(("parallel", "parallel")),
    )(x, meta_tokens)


def _mm_kernel(x_ref, w_ref, o_ref):
    o_ref[...] = _dot(x_ref[...], w_ref[...]).astype(o_ref.dtype)


def _matmul(x, w, layer, out_dtype):
    m, k = x.shape
    n = w.shape[2]
    tm = _pick(m, (2048, 1024, 512, 256, 128))
    tn = _pick(n, (1024, 512, 256, 128))
    return pl.pallas_call(
        _mm_kernel,
        grid=(m // tm, n // tn),
        in_specs=[pl.BlockSpec((tm, k), lambda i, j: (i, 0)),
                  pl.BlockSpec((None, k, tn), lambda i, j: (layer, 0, j))],
        out_specs=pl.BlockSpec((tm, tn), lambda i, j: (i, j)),
        out_shape=jax.ShapeDtypeStruct((m, n), out_dtype),
        compiler_params=_cparams(("parallel", "parallel")),
    )(x, w)


def _sb_blocks(q_ref, k_ref, v_ref, tri, heads, k0, n_blocks, valid, carries):
    tq, d = SB_TQ, SB_HEAD_DIM
    scale = d ** -0.5
    span = pl.ds(k0, n_blocks * tq)
    cols = [slice(h * d, (h + 1) * d) for h in range(heads)]
    z = [_dot_nt(q_ref[0, :, cs], k_ref[0, span, cs]) * scale for cs in cols]
    log_beta, sums = [], []
    for h in range(heads):
        for j in range(n_blocks):
            zj = z[h][:, j * tq:(j + 1) * tq]
            sp = jnp.maximum(zj, 0.0) + jnp.log(1.0 + jnp.exp(-jnp.abs(zj)))
            log_beta.append(zj - sp)
            sums.append(_dot(jnp.where(valid[j], sp, 0.0).astype(bf16), tri))
    outs, new_carries = [], []
    for h in range(heads):
        carry = carries[h]
        w = [None] * n_blocks
        for j in reversed(range(n_blocks)):
            sm = sums[h * n_blocks + j]
            between = sm[:, :tq] + carry
            w[j] = jnp.where(valid[j], jnp.exp(log_beta[h * n_blocks + j] - between), 0.0).astype(bf16)
            carry = carry + sm[:, tq:]
        outs.append(_dot(w[0] if n_blocks == 1 else jnp.concatenate(w, axis=1), v_ref[0, span, cols[h]]))
        new_carries.append(carry)
    return outs, new_carries


def _sb_kernel(q_ref, k_ref, v_ref, tri_ref, g_ref, o_ref, *, pad, heads):
    tq, d = SB_TQ, SB_HEAD_DIM
    i = pl.program_id(2)
    t0 = i * tq
    wb = jnp.maximum(i - (SB_WINDOW - 1), 0)
    ws = pl.multiple_of(wb * tq, tq)
    key_minus_query = lax.broadcasted_iota(i32, (tq, tq), 1) - lax.broadcasted_iota(i32, (tq, tq), 0)
    col = lax.broadcasted_iota(i32, (tq, tq), 1)
    tri = tri_ref[...]

    def valid_for(k0):
        return jnp.logical_and(key_minus_query < t0 - k0, col >= pad - k0)

    def lowest(carries):
        m = jnp.min(carries[0])
        for c in carries[1:]:
            m = jnp.minimum(m, jnp.min(c))
        return m

    zero = [jnp.zeros((tq, tq), f32)] * heads
    accs, carries = _sb_blocks(q_ref, k_ref, v_ref, tri, heads, ws, SB_WINDOW,
                               [valid_for(ws + j * tq) for j in range(SB_WINDOW)], zero)

    def cond(c):
        jb, cmin, _, _ = c
        return jnp.logical_and(jb >= 0, cmin < SB_DEAD_LOGW)

    def body(c):
        jb, _, accs, carries = c
        k0 = pl.multiple_of(jb * tq, tq)
        outs, carries = _sb_blocks(q_ref, k_ref, v_ref, tri, heads, k0, 1, [valid_for(k0)], list(carries))
        return jb - 1, lowest(carries), tuple(a + o for a, o in zip(accs, outs)), tuple(carries)

    _, _, accs, _ = lax.while_loop(cond, body, (wb - 1, lowest(carries), tuple(accs), tuple(carries)))
    for h in range(heads):
        cs = slice(h * d, (h + 1) * d)
        ms = jnp.mean(accs[h] * accs[h], axis=1, keepdims=True)
        o_ref[0, :, cs] = (accs[h] * lax.rsqrt(ms + RMS_EPS) * g_ref[:, cs]).astype(o_ref.dtype)


def _sb_attention(proj, norm_g, pad):
    b, lp, _ = proj.shape
    tq, heads = SB_TQ, SB_HEADS_PER_STEP
    wid = heads * SB_HEAD_DIM
    groups = SB_HEADS // heads
    later = jnp.arange(tq)[:, None] > jnp.arange(tq)[None, :]
    tri = jnp.concatenate([later, jnp.ones((tq, tq), bool)], axis=1).astype(bf16)
    return pl.pallas_call(
        functools.partial(_sb_kernel, pad=pad, heads=heads),
        grid=(b, groups, lp // tq),
        in_specs=[pl.BlockSpec((1, tq, wid), lambda bi, h, i: (bi, i, h)),
                  pl.BlockSpec((1, lp, wid), lambda bi, h, i: (bi, 0, groups + h)),
                  pl.BlockSpec((1, lp, wid), lambda bi, h, i: (bi, 0, 2 * groups + h)),
                  pl.BlockSpec((tq, 2 * tq), lambda bi, h, i: (0, 0)),
                  pl.BlockSpec((1, wid), lambda bi, h, i: (0, h))],
        out_specs=pl.BlockSpec((1, tq, wid), lambda bi, h, i: (bi, i, h)),
        out_shape=jax.ShapeDtypeStruct((b, lp, SB_HEADS * SB_HEAD_DIM), bf16),
        compiler_params=_cparams(("parallel", "parallel", "parallel")),
    )(proj, proj, proj, tri, norm_g)


def _gla_kernel(q_ref, k_ref, v_ref, x_ref, wlr_ref, wg_ref, bg_ref, go_ref, gn_ref, o_ref, st_ref, *, pad):
    c, rows, dk, dv = GLA_CHUNK, GLA_ROWS, GLA_DK, GLA_DV
    s = pl.program_id(1)

    @pl.when(s == 0)
    def _():
        st_ref[...] = jnp.zeros_like(st_ref)

    real = (s * rows + lax.broadcasted_iota(i32, (rows, 1), 0)) >= pad
    r_i = lax.broadcasted_iota(i32, (rows, rows), 0)
    c_i = lax.broadcasted_iota(i32, (rows, rows), 1)
    shift = c.bit_length() - 1
    causal = jnp.logical_and(c_i <= r_i, (r_i >> shift) == (c_i >> shift))
    causal_b = causal.astype(bf16)
    scale = dk ** -0.5

    heads = range(GLA_HEADS)
    ks = [slice(h * dk, (h + 1) * dk) for h in heads]
    vs = [slice(h * dv, (h + 1) * dv) for h in heads]
    g_lr = _dot(x_ref[0], wlr_ref[...]).astype(bf16)
    gate_pre = _dot(g_lr, wg_ref[...]) + bg_ref[...]
    log_a = (jnp.minimum(gate_pre, 0.0) - jnp.log(1.0 + jnp.exp(-jnp.abs(gate_pre)))) * (1.0 / GLA_GATE_TAU)
    bcum = []
    for h in heads:
        g = jnp.where(real, log_a[:, ks[h]], 0.0)
        g_hi = g.astype(bf16)
        g_lo = (g - g_hi.astype(f32)).astype(bf16)
        bcum.append(_dot(causal_b, g_hi) + _dot(causal_b, g_lo))
    kf = [jnp.where(real, k_ref[0, :, ks[h]].astype(f32), 0.0) for h in heads]
    q_dec = [(q_ref[0, :, ks[h]].astype(f32) * scale * jnp.exp(bcum[h])).astype(bf16) for h in heads]
    scores = [jnp.where(causal, _dot_nt(q_dec[h], (kf[h] * jnp.exp(-bcum[h])).astype(bf16)), 0.0) for h in heads]
    o_intra = [_dot(scores[h].astype(bf16), v_ref[0, :, vs[h]]) for h in heads]
    st = [st_ref[h] for h in heads]
    o_inter = [[] for _ in heads]
    for n in range(rows // c):
        rs = slice(n * c, (n + 1) * c)
        for h in heads:
            b_last = bcum[h][(n + 1) * c - 1:(n + 1) * c, :]
            k_dec = (kf[h][rs] * jnp.exp(b_last - bcum[h][rs])).astype(bf16)
            o_inter[h].append(_dot_nt(q_dec[h][rs], st[h].astype(bf16)))
            st[h] = st[h] * jnp.exp(b_last) + _dot_tn(v_ref[0, rs, vs[h]], k_dec)
    for h in heads:
        st_ref[h] = st[h]
        o = o_intra[h] + jnp.concatenate(o_inter[h], axis=0)
        ms = jnp.mean(o * o, axis=1, keepdims=True)
        gate = go_ref[0, :, vs[h]].astype(f32)
        y = o * lax.rsqrt(ms + RMS_EPS) * gn_ref[:, vs[h]] * (gate * jax.nn.sigmoid(gate))
        o_ref[0, :, vs[h]] = y.astype(o_ref.dtype)


def _gla(proj, x, w_lr, w_g, b_g, norm_g, pad):
    b, lp, _ = proj.shape
    d = x.shape[2]
    rows = GLA_ROWS
    kw, vw = GLA_HEADS * GLA_DK, GLA_HEADS * GLA_DV
    fix = lambda bi, s: (0, 0)
    return pl.pallas_call(
        functools.partial(_gla_kernel, pad=pad),
        grid=(b, lp // rows),
        in_specs=[pl.BlockSpec((1, rows, kw), lambda bi, s: (bi, s, 6)),
                  pl.BlockSpec((1, rows, kw), lambda bi, s: (bi, s, 7)),
                  pl.BlockSpec((1, rows, vw), lambda bi, s: (bi, s, 4)),
                  pl.BlockSpec((1, rows, d), lambda bi, s: (bi, s, 0)),
                  pl.BlockSpec((d, LANES), fix), pl.BlockSpec((LANES, kw), fix), pl.BlockSpec((1, kw), fix),
                  pl.BlockSpec((1, rows, vw), lambda bi, s: (bi, s, 5)),
                  pl.BlockSpec((1, vw), fix)],
        out_specs=pl.BlockSpec((1, rows, vw), lambda bi, s: (bi, s, 0)),
        out_shape=jax.ShapeDtypeStruct((b, lp, vw), bf16),
        scratch_shapes=[pltpu.VMEM((GLA_HEADS, GLA_DV, GLA_DK), f32)],
        compiler_params=_cparams(("parallel", "arbitrary")),
    )(proj, proj, proj, x, w_lr, w_g, b_g, proj, norm_g)


def _layer_norm_rows(r, g, b):
    mu = jnp.mean(r, axis=-1, keepdims=True)
    d = r - mu
    var = jnp.mean(d * d, axis=-1, keepdims=True)
    return d * lax.rsqrt(var + LN_EPS) * g + b


def _route(logits):
    m = jnp.max(logits, axis=0, keepdims=True)
    e = jnp.exp(logits - m)
    aff = e / jnp.sum(e, axis=0, keepdims=True)
    rows = [aff[j:j + 1, :] for j in range(N_EXPERTS)]
    scores = []
    for g in range(N_GROUPS):
        a, b, c, d = rows[4 * g:4 * g + 4]
        hi1, lo1 = jnp.maximum(a, b), jnp.minimum(a, b)
        hi2, lo2 = jnp.maximum(c, d), jnp.minimum(c, d)
        scores.append(jnp.maximum(hi1, hi2) + jnp.maximum(jnp.minimum(hi1, hi2), jnp.maximum(lo1, lo2)))
    best = scores[0]
    g_sel = jnp.zeros(best.shape, i32)
    for g in range(1, N_GROUPS):
        better = scores[g] > best
        g_sel = jnp.where(better, g, g_sel)
        best = jnp.where(better, scores[g], best)
    sel = list(rows[0:4])
    for g in range(1, N_GROUPS):
        here = g_sel == g
        sel = [jnp.where(here, rows[4 * g + j], sel[j]) for j in range(4)]
    v1, i1 = sel[0], jnp.zeros(best.shape, i32)
    for j in range(1, 4):
        gt = sel[j] > v1
        i1 = jnp.where(gt, j, i1)
        v1 = jnp.where(gt, sel[j], v1)
    v2, i2 = jnp.full(best.shape, -1.0, f32), jnp.zeros(best.shape, i32)
    for j in range(4):
        cand = jnp.where(i1 == j, -1.0, sel[j])
        gt = cand > v2
        i2 = jnp.where(gt, j, i2)
        v2 = jnp.where(gt, cand, v2)
    s = v1 + v2
    top_e = jnp.concatenate([4 * g_sel + i1, 4 * g_sel + i2], axis=0)
    top_w = jnp.concatenate([v1 / s, v2 / s], axis=0)
    return top_e, top_w


def _oproj_kernel(ysb_ref, ygla_ref, w1_ref, w2_ref, h_ref, g_ref, b_ref, wr_ref, br_ref,
                  x1p_ref, te_ref, tw_ref, *, alpha):
    tm, d = h_ref.shape
    n = tm // OPROJ_ROWS
    groups = [slice(p * OPROJ_ROWS, (p + 1) * OPROJ_ROWS) for p in range(n)]
    mixes = [_dot(ysb_ref[rs, :], w1_ref[...]) + _dot(ygla_ref[rs, :], w2_ref[...]) for rs in groups]
    wr = wr_ref[...]
    for p, rs in enumerate(groups):
        x1 = _layer_norm_rows(alpha * h_ref[rs, :].astype(f32) + mixes[p], g_ref[...], b_ref[...])
        _store_token_rows(x1p_ref, x1, p * OPROJ_ROWS)
        l = _dot(x1.astype(bf16), wr)
        l = l + pltpu.roll(l, LANES - N_EXPERTS, 1)
        logits = l.T[:N_EXPERTS] + br_ref[...]
        top_e, top_w = _route(logits)
        te_ref[:, rs] = top_e
        tw_ref[:, rs] = top_w


def _oproj_ln_route(y_sb, y_gla, w_out, layer, h, ln_g, ln_b, wr, br, alpha):
    m, d = h.shape
    kh = y_sb.shape[1]
    tm = _pick(m, (512, 256, 128))
    row = lambda i: (i, 0)
    fix = lambda i: (0, 0)
    col = lambda i: (0, i)
    return pl.pallas_call(
        functools.partial(_oproj_kernel, alpha=alpha),
        grid=(m // tm,),
        in_specs=[pl.BlockSpec((tm, kh), row), pl.BlockSpec((tm, kh), row),
                  pl.BlockSpec((None, kh, d), lambda i: (layer, 0, 0)),
                  pl.BlockSpec((None, kh, d), lambda i: (layer, 1, 0)),
                  pl.BlockSpec((tm, d), row),
                  pl.BlockSpec((1, d), fix), pl.BlockSpec((1, d), fix),
                  pl.BlockSpec((d, LANES), fix), pl.BlockSpec((N_EXPERTS, 1), fix)],
        out_specs=[pl.BlockSpec((tm * ROW_TILE, LANES), row),
                   pl.BlockSpec((2, tm), col), pl.BlockSpec((2, tm), col)],
        out_shape=[jax.ShapeDtypeStruct((m * ROW_TILE, LANES), f32),
                   jax.ShapeDtypeStruct((2, m), i32), jax.ShapeDtypeStruct((2, m), f32)],
        compiler_params=_cparams(("parallel",)),
    )(y_sb, y_gla, w_out, w_out, h, ln_g, ln_b, wr, br)


def _rank_kernel(e_ref, rank_ref, cnt_ref, carry_ref, *, chunk):
    @pl.when(pl.program_id(0) == 0)
    def _():
        carry_ref[...] = jnp.zeros_like(carry_ref)

    e = e_ref[...]
    onehot = lax.broadcasted_iota(i32, (N_EXPERTS, chunk), 0) == e
    before = (lax.broadcasted_iota(i32, (chunk, chunk), 0)
              < lax.broadcasted_iota(i32, (chunk, chunk), 1)).astype(bf16)
    seen = _dot(onehot.astype(bf16), before) + carry_ref[...]
    rank_ref[...] = jnp.sum(jnp.where(onehot, seen, 0.0), axis=0, keepdims=True).astype(i32)
    total = carry_ref[...] + jnp.sum(onehot.astype(f32), axis=1, keepdims=True)
    carry_ref[...] = total
    cnt_ref[...] = total


def _expert_ranks(e_flat):
    n = e_flat.shape[1]
    chunk = _pick(n, (RANK_CHUNK, 256, 128))
    return pl.pallas_call(
        functools.partial(_rank_kernel, chunk=chunk),
        grid=(n // chunk,),
        in_specs=[pl.BlockSpec((1, chunk), lambda i: (0, i))],
        out_specs=[pl.BlockSpec((1, chunk), lambda i: (0, i)),
                   pl.BlockSpec((N_EXPERTS, 1), lambda i: (0, 0))],
        out_shape=[jax.ShapeDtypeStruct((1, n), i32), jax.ShapeDtypeStruct((N_EXPERTS, 1), f32)],
        scratch_shapes=[pltpu.VMEM((N_EXPERTS, 1), f32)],
        compiler_params=_cparams(("arbitrary",)),
    )(e_flat)


def _row_copy(src, dst, sem):
    return pltpu.make_async_copy(src, dst, sem)


def _token_rows(ref, t, n=1):
    start = t * ROW_TILE
    if not isinstance(start, int):
        start = pl.multiple_of(start, ROW_TILE)
    return ref.at[pl.ds(start, n * ROW_TILE)]


ZERO_CHUNK = 64


def _dispatch_kernel(pads_ref, dest_hbm, x_ref, xs_hbm, dsm, zbuf, sem_i, sem_d, sem_z, *, tm, n_tiles):
    i = pl.program_id(0)
    slot = i % 2

    def idx_copy(t, sl):
        return _row_copy(dest_hbm.at[t], dsm.at[pl.ds(sl, 1)], sem_i.at[sl])

    def zero_copy(first, n):
        return _row_copy(_token_rows(zbuf, 0, n), _token_rows(xs_hbm, first, n), sem_z)

    @pl.when(i == 0)
    def _():
        idx_copy(0, 0).start()
        zbuf[...] = jnp.zeros_like(zbuf)
        n_big, n_one = 0, 0
        n_runs = pads_ref.shape[0] // 2
        for e in range(n_runs):
            first, count = pads_ref[e], pads_ref[n_runs + e]
            big = count >> (ZERO_CHUNK.bit_length() - 1)
            rest = count - big * ZERO_CHUNK

            def chunk(c, carry, first=first):
                zero_copy(first + c * ZERO_CHUNK, ZERO_CHUNK).start()
                return carry

            def single(r, carry, first=first + big * ZERO_CHUNK):
                zero_copy(first + r, 1).start()
                return carry

            lax.fori_loop(0, big, chunk, 0)
            lax.fori_loop(0, rest, single, 0)
            n_big, n_one = n_big + big, n_one + rest
        lax.fori_loop(0, n_big, lambda c, carry: (zero_copy(0, ZERO_CHUNK).wait(), carry)[1], 0)
        lax.fori_loop(0, n_one, lambda c, carry: (zero_copy(0, 1).wait(), carry)[1], 0)

    idx_copy(i, slot).wait()

    @pl.when(i + 1 < n_tiles)
    def _():
        idx_copy(i + 1, 1 - slot).start()

    for r in range(tm):
        for k in range(2):
            d = dsm[slot, k * tm + r]
            _row_copy(_token_rows(x_ref, r), _token_rows(xs_hbm, d), sem_d).start(priority=k)
    for _ in range(2):
        _row_copy(x_ref, _token_rows(xs_hbm, 0, tm), sem_d).wait()


def _dispatch(x_rows, dest_tiles, pads, n_slots):
    n_tiles = dest_tiles.shape[0]
    tm = dest_tiles.shape[2] // 2
    grid_spec = pltpu.PrefetchScalarGridSpec(
        num_scalar_prefetch=1,
        grid=(n_tiles,),
        in_specs=[pl.BlockSpec(memory_space=pl.ANY),
                  pl.BlockSpec((tm * ROW_TILE, LANES), lambda i, pads: (i, 0))],
        out_specs=pl.BlockSpec(memory_space=pl.ANY),
        scratch_shapes=[pltpu.SMEM((2, 2 * tm), i32), pltpu.VMEM((ZERO_CHUNK * ROW_TILE, LANES), f32),
                        pltpu.SemaphoreType.DMA((2,)), pltpu.SemaphoreType.DMA(()), pltpu.SemaphoreType.DMA(())],
    )
    return pl.pallas_call(
        functools.partial(_dispatch_kernel, tm=tm, n_tiles=n_tiles),
        grid_spec=grid_spec,
        out_shape=jax.ShapeDtypeStruct((n_slots * ROW_TILE, LANES), f32),
        compiler_params=_cparams(("arbitrary",)),
    )(pads, dest_tiles, x_rows)


def _ffn_kernel(be_ref, na_ref, xs_ref, wg_ref, wu_ref, wd_ref, y_ref):
    del be_ref
    blk = pl.program_id(0)

    @pl.when(blk < na_ref[0])
    def _():
        x = jnp.concatenate([w.astype(bf16) for w in _load_token_rows(xs_ref, MOE_BLK)], axis=1)
        gate = _dot(x, wg_ref[0])
        up = _dot(x, wu_ref[0])
        hid = (gate * jax.nn.sigmoid(gate) * up).astype(bf16)
        _store_token_rows(y_ref, _dot(hid, wd_ref[0]))

    @pl.when(blk >= na_ref[0])
    def _():
        y_ref[...] = jnp.zeros_like(y_ref)


def _expert_ffn(xs, block_e, n_active, wg, wu, wd, layer):
    d, dff = wg.shape[2], wg.shape[3]
    n_blocks = xs.shape[0] // (MOE_BLK * ROW_TILE)
    grid_spec = pltpu.PrefetchScalarGridSpec(
        num_scalar_prefetch=2,
        grid=(n_blocks,),
        in_specs=[pl.BlockSpec((MOE_BLK * ROW_TILE, LANES), lambda b, be, na: (jnp.minimum(b, na[0] - 1), 0)),
                  pl.BlockSpec((None, 1, d, dff), lambda b, be, na: (layer, be[b], 0, 0)),
                  pl.BlockSpec((None, 1, d, dff), lambda b, be, na: (layer, be[b], 0, 0)),
                  pl.BlockSpec((None, 1, dff, d), lambda b, be, na: (layer, be[b], 0, 0))],
        out_specs=pl.BlockSpec((MOE_BLK * ROW_TILE, LANES), lambda b, be, na: (b, 0)),
    )
    return pl.pallas_call(
        _ffn_kernel,
        grid_spec=grid_spec,
        out_shape=jax.ShapeDtypeStruct(xs.shape, f32),
        compiler_params=_cparams(("arbitrary",)),
    )(block_e, n_active, xs, wg, wu, wd)


def _combine_kernel(first_hbm, pairs_hbm, x1p_ref, tw_ref, g_ref, b_ref, y_hbm, out_ref,
                    ism, buf_a, buf_b, sem_i, sem_a, sem_b, *, tm, n_steps, alpha):
    j = pl.program_id(0)
    s = j % 2

    def pair_copy(step, sl):
        return _row_copy(pairs_hbm.at[step], ism.at[pl.ds(sl, 1)], sem_i.at[sl])

    def issue(buf, sem, sl, off):
        for r in range(tm):
            for k in range(2):
                d = ism[sl, off + k * tm + r]
                _row_copy(_token_rows(y_hbm, d), _token_rows(buf.at[k], r), sem).start(priority=k)

    def wait_buf(buf, sem):
        for k in range(2):
            _row_copy(_token_rows(y_hbm, 0, tm), buf.at[k], sem).wait()

    def finish(buf, t0):
        rs = slice(t0, t0 + tm)
        w0, w1 = tw_ref[rs, 0:1], tw_ref[rs, 1:2]
        rows0 = _load_token_rows(buf.at[0], tm)
        rows1 = _load_token_rows(buf.at[1], tm)
        rows_x = _load_token_rows(_token_rows(x1p_ref, t0, tm), tm)
        res = [alpha * rows_x[c] + (w0 * rows0[c] + w1 * rows1[c]) for c in range(ROW_TILE)]
        inv_d = 1.0 / (ROW_TILE * LANES)
        mu = jnp.sum(functools.reduce(jnp.add, res), axis=-1, keepdims=True) * inv_d
        res = [r - mu for r in res]
        var = jnp.sum(functools.reduce(jnp.add, [r * r for r in res]), axis=-1, keepdims=True) * inv_d
        rstd = lax.rsqrt(var + LN_EPS)
        for c, r in enumerate(res):
            cs = slice(c * LANES, (c + 1) * LANES)
            out_ref[rs, cs] = (r * rstd * g_ref[:, cs] + b_ref[:, cs]).astype(out_ref.dtype)

    @pl.when(j == 0)
    def _():
        first = _row_copy(first_hbm.at[0], ism.at[pl.ds(0, 1), pl.ds(0, 2 * tm)], sem_i.at[0])
        first.start()
        first.wait()

        def body(r, carry):
            for k in range(2):
                d = ism[0, k * tm + r]
                _row_copy(_token_rows(y_hbm, d), _token_rows(buf_a.at[k], r), sem_a).start(priority=k)
            return carry

        lax.fori_loop(0, tm, body, 0, unroll=8)
        pair_copy(0, 0).start()

    pair_copy(j, s).wait()

    @pl.when(j + 1 < n_steps)
    def _():
        pair_copy(j + 1, 1 - s).start()

    wait_buf(buf_a, sem_a)
    issue(buf_b, sem_b, s, 0)
    finish(buf_a, 0)
    wait_buf(buf_b, sem_b)
    issue(buf_a, sem_a, s, 2 * tm)
    finish(buf_b, tm)

    @pl.when(j == n_steps - 1)
    def _():
        wait_buf(buf_a, sem_a)


def _combine(dest_tiles, x1p, tw_t, ln_g, ln_b, y, alpha, out_dtype):
    m, d = tw_t.shape[0], ln_g.shape[1]
    n_tiles = dest_tiles.shape[0]
    tm = dest_tiles.shape[2] // 2
    assert n_tiles % 2 == 0
    n_steps = n_tiles // 2
    nxt = jnp.minimum(2 * jnp.arange(n_steps) + 2, n_tiles - 1)
    pairs = jnp.concatenate([dest_tiles[1::2], dest_tiles[nxt]], axis=2)
    row = lambda i: (i, 0)
    fix = lambda i: (0, 0)
    half_buf = pltpu.VMEM((2, tm * ROW_TILE, LANES), f32)
    return pl.pallas_call(
        functools.partial(_combine_kernel, tm=tm, n_steps=n_steps, alpha=alpha),
        grid=(n_steps,),
        in_specs=[pl.BlockSpec(memory_space=pl.ANY), pl.BlockSpec(memory_space=pl.ANY),
                  pl.BlockSpec((2 * tm * ROW_TILE, LANES), row),
                  pl.BlockSpec((2 * tm, 2), row),
                  pl.BlockSpec((1, d), fix), pl.BlockSpec((1, d), fix),
                  pl.BlockSpec(memory_space=pl.ANY)],
        out_specs=pl.BlockSpec((2 * tm, d), row),
        out_shape=jax.ShapeDtypeStruct((m, d), out_dtype),
        scratch_shapes=[pltpu.SMEM((2, 4 * tm), i32), half_buf, half_buf,
                        pltpu.SemaphoreType.DMA((2,)), pltpu.SemaphoreType.DMA(()), pltpu.SemaphoreType.DMA(())],
        compiler_params=_cparams(("arbitrary",)),
    )(dest_tiles, pairs, x1p, tw_t, ln_g, ln_b, y)


def _moe_plan(top_e, n_tok):
    rank, counts = _expert_ranks(top_e.reshape(1, 2 * n_tok))
    counts = counts.reshape(N_EXPERTS).astype(i32)
    padded = (counts + MOE_BLK - 1) // MOE_BLK * MOE_BLK
    pend = jnp.cumsum(padded)
    pstart = pend - padded
    e_flat = top_e.reshape(2 * n_tok)
    start_of = jnp.sum(jnp.where(e_flat[:, None] == jnp.arange(N_EXPERTS, dtype=i32)[None, :],
                                 pstart[None, :], 0), axis=1)
    dest = (start_of + rank.reshape(2 * n_tok)).reshape(2, n_tok)
    n_blocks = -(-(2 * n_tok) // MOE_BLK) + N_EXPERTS
    blk_row = jnp.arange(n_blocks, dtype=i32) * MOE_BLK
    block_e = jnp.minimum(jnp.sum((blk_row[:, None] >= pend[None, :]).astype(i32), axis=1), N_EXPERTS - 1)
    n_active = (pend[-1] // MOE_BLK).reshape(1).astype(i32)
    tm = _pick(n_tok, (TOK_TILE, 128))
    dest_tiles = dest.reshape(2, n_tok // tm, tm).transpose(1, 0, 2).reshape(n_tok // tm, 1, 2 * tm)
    n_slots = n_blocks * MOE_BLK
    pads = jnp.concatenate([pstart + counts, pend[-1:], padded - counts, n_slots - pend[-1:]]).astype(i32)
    return dest_tiles, block_e.astype(i32), n_active, pads, n_slots


def kernel(x, meta_tokens, w_in, w_gate_lr, b_gate_lr, sb_out_norm, gla_out_norm, w_out, ln1_g, ln1_b,
           w_router, b_router, w_ff_gate, w_ff_up, w_ff_down, ln2_g, ln2_b):
    bsz, seq, d = x.shape
    depth = w_in.shape[0]
    alpha = float((2 * depth) ** 0.25)
    length = N_META + seq
    lp = -(-length // SEQ_ALIGN) * SEQ_ALIGN
    pad = lp - length
    assert (pad - (-length) % GLA_CHUNK) % GLA_CHUNK == 0
    n_tok = bsz * lp

    if seq % (pad + N_META) == 0 and pad % 16 == 0:
        h16 = _embed(x, meta_tokens, pad).reshape(n_tok, d)
    else:
        meta = jnp.broadcast_to(meta_tokens[None].astype(bf16), (bsz, N_META, d))
        h16 = jnp.concatenate([jnp.zeros((bsz, pad, d), bf16), meta, x.astype(bf16)], axis=1).reshape(n_tok, d)

    sb_w = SB_HEADS * SB_HEAD_DIM
    gla_kw = GLA_HEADS * GLA_DK
    gla_vw = GLA_HEADS * GLA_DV
    o_lr = 3 * sb_w + 2 * gla_kw + gla_vw
    o_go = o_lr + GLA_GATE_RANK
    w_main = jnp.concatenate([w_in[:, :, :o_lr], w_in[:, :, o_go:]], axis=2).astype(bf16)
    w_lr = jnp.pad(w_in[:, :, o_lr:o_go], ((0, 0), (0, 0), (0, LANES - GLA_GATE_RANK))).astype(bf16)
    w_g = jnp.pad(w_gate_lr, ((0, 0), (0, LANES - GLA_GATE_RANK), (0, 0))).astype(bf16)
    w_out16 = w_out.astype(bf16)
    wr_hi = w_router.astype(bf16)
    wr_lo = (w_router - wr_hi.astype(f32)).astype(bf16)
    wr = jnp.pad(jnp.concatenate([wr_hi, wr_lo], axis=1), ((0, 0), (0, LANES - 2 * N_EXPERTS)))
    br = b_router.astype(f32).reshape(N_EXPERTS, 1)
    wg16, wu16, wd16 = w_ff_gate.astype(bf16), w_ff_up.astype(bf16), w_ff_down.astype(bf16)

    for l in range(depth):
        proj = _matmul(h16, w_main, l, bf16)
        proj3 = proj.reshape(bsz, lp, proj.shape[1])
        y_sb = _sb_attention(proj3, sb_out_norm[l].reshape(1, sb_w), pad)
        y_gla = _gla(proj3, h16.reshape(bsz, lp, d), w_lr[l], w_g[l], b_gate_lr[l].reshape(1, gla_kw),
                     gla_out_norm[l].reshape(1, gla_vw), pad)
        x1p, top_e, top_w = _oproj_ln_route(
            y_sb.reshape(n_tok, sb_w), y_gla.reshape(n_tok, gla_vw), w_out16, l,
            h16, ln1_g[l].reshape(1, d), ln1_b[l].reshape(1, d), wr, br, alpha)
        dest_tiles, block_e, n_active, pads, n_rows = _moe_plan(top_e, n_tok)
        xs = _dispatch(x1p, dest_tiles, pads, n_rows)
        y = _expert_ffn(xs, block_e, n_active, wg16, wu16, wd16, l)
        h16 = _combine(dest_tiles, x1p, top_w.T, ln2_g[l].reshape(1, d), ln2_b[l].reshape(1, d), y, alpha,
                       bf16 if l + 1 < depth else x.dtype)

    return h16.reshape(bsz, lp, d)[:, pad + N_META:, :]
```
